```python
import math
import jax
import jax.numpy as jnp
from jax import lax
import numpy as np

D_MODEL = 2048
BATCH = 8
SEQ = 4096
DEPTH = 4

BLOCK = 128

DILATED_GROUPS = ((128, 1), (512, 4), (2048, 16))
N_GROUPS_A = 3
HEADS_PER_GROUP_A = 4
HEAD_DIM_A = 128
QKV_A = N_GROUPS_A * HEADS_PER_GROUP_A * HEAD_DIM_A
HEADS_B = 4
HEAD_DIM_B = 64
QK_B = HEADS_B * 2 * HEAD_DIM_B
V_B = HEADS_B * 2 * HEAD_DIM_B
HEADS_C = 8
KV_HEADS_C = 2
GQA_GROUP_C = HEADS_C // KV_HEADS_C
HEAD_DIM_C = 64
WINDOW_C = 128
Q_C = HEADS_C * HEAD_DIM_C
KV_C = KV_HEADS_C * HEAD_DIM_C

IN_SPLITS = (QKV_A, QKV_A, QKV_A, QK_B, QK_B, V_B, Q_C, KV_C, KV_C)
IN_COLS = 3 * QKV_A + 2 * QK_B + V_B + Q_C + 2 * KV_C
N_BRANCH = 3
BRANCH_WIDTH = 512

N_EXPERT_GROUPS = 4
EXPERTS_PER_GROUP = 8
N_EXPERTS = N_EXPERT_GROUPS * EXPERTS_PER_GROUP
TOP_K_EXPERTS = 2
D_EXPERT = 512

LN_EPS = 1e-5
DEEPNORM_ALPHA = (2 * DEPTH) ** 0.25
DEEPNORM_BETA = (8 * DEPTH) ** -0.25

kernel_name = 'hybrid_dilated_diff_swa_hmoe_deepnorm'


def alibi_slopes(n):
    return jnp.asarray(2.0 ** (-8.0 * np.arange(1, n + 1, dtype=np.float32) / n), dtype=jnp.float32)


def layer_norm(x, g, b):
    x32 = x.astype(jnp.float32)
    mu = jnp.mean(x32, axis=-1, keepdims=True)
    var = jnp.mean(jnp.square(x32 - mu), axis=-1, keepdims=True)
    y = (x32 - mu) * lax.rsqrt(var + LN_EPS) * g.astype(jnp.float32) + b.astype(jnp.float32)
    return y.astype(x.dtype)


def _with_prev_block(x, blk):
    b, h, l, d = x.shape
    xb = x.reshape(b, h, l // blk, blk, d)
    prev = jnp.pad(xb, ((0, 0), (0, 0), (1, 0), (0, 0), (0, 0)))[:, :, :-1]
    return jnp.concatenate([prev, xb], axis=3)


def banded_logits(q, k, max_dist, dist_unit, slopes):
    b, h, g, l, d = q.shape
    nb = l // BLOCK
    qb = q.reshape(b, h, g, nb, BLOCK, d)
    kb = _with_prev_block(k, BLOCK)
    s = jnp.einsum('bhgnqd,bhnkd->bhgnqk', qb, kb).astype(jnp.float32) * (d ** -0.5)
    dist = jnp.arange(BLOCK)[:, None] + BLOCK - jnp.arange(2 * BLOCK)[None, :]
    key_pos = jnp.arange(nb)[:, None] * BLOCK - BLOCK + jnp.arange(2 * BLOCK)[None, :]
    valid = (dist >= 0) & (dist <= max_dist) & (key_pos[:, None, :] >= 0)
    bias = -slopes[None, :, :, None, None, None] * (dist * dist_unit).astype(jnp.float32)
    return jnp.where(valid, s + bias, -jnp.inf)


def banded_values(p, v):
    b, h, g, nb, blk, _ = p.shape
    vb = _with_prev_block(v, blk)
    o = jnp.einsum('bhgnqk,bhnkd->bhgnqd', p.astype(v.dtype), vb)
    return o.reshape(b, h, g, nb * blk, v.shape[-1])


def dilated_group_attention(q, k, v, window, dilation, slopes):
    b, s_len, h, d = q.shape
    r = dilation
    l_sub = s_len // r
    l_pad = -(-l_sub // BLOCK) * BLOCK

    def to_sub(t):
        t = t.reshape(b, l_sub, r, h, d).transpose(0, 3, 2, 1, 4).reshape(b, h * r, l_sub, d)
        return jnp.pad(t, ((0, 0), (0, 0), (0, l_pad - l_sub), (0, 0)))

    qs, ks, vs = to_sub(q), to_sub(k), to_sub(v)
    sl = jnp.repeat(slopes, r)[:, None]
    logits = banded_logits(qs[:, :, None], ks, window // r, r, sl)
    lse = jax.nn.logsumexp(logits, axis=-1)
    p = jnp.exp(logits - lse[..., None])
    o = banded_values(p, vs)[:, :, 0, :l_sub]
    o = o.reshape(b, h, r, l_sub, d).transpose(0, 3, 2, 1, 4).reshape(b, s_len, h, d)
    lse = lse.reshape(b, h * r, l_pad)[:, :, :l_sub]
    lse = lse.reshape(b, h, r, l_sub).transpose(0, 3, 2, 1).reshape(b, s_len, h)
    return o, lse


def mixer_dilated(q, k, v, slopes):
    b, s_len = q.shape[:2]
    outs, lses = [], []
    for gi, (window, dilation) in enumerate(DILATED_GROUPS):
        o, l = dilated_group_attention(q[:, :, gi], k[:, :, gi], v[:, :, gi], window, dilation, slopes[gi])
        outs.append(o)
        lses.append(l)
    o = jnp.stack(outs, axis=2)
    wts = jax.nn.softmax(jnp.stack(lses, axis=2), axis=2)
    y = jnp.sum(wts[..., None].astype(o.dtype) * o, axis=2)
    return y.reshape(b, s_len, HEADS_PER_GROUP_A * HEAD_DIM_A)


def mixer_diff(q, k, v, lam_vecs, subln_g, lambda_init, slopes):
    b, s_len, h, _, d = q.shape
    nb = s_len // BLOCK
    lv = lam_vecs.astype(jnp.float32)
    lam = jnp.exp(jnp.sum(lv[0] * lv[1])) - jnp.exp(jnp.sum(lv[2] * lv[3])) + lambda_init
    kt = k.transpose(0, 2, 3, 1, 4)
    vt = v.transpose(0, 2, 1, 3)
    qb = q.reshape(b, nb, BLOCK, h, 2, d).transpose(1, 0, 3, 4, 2, 5)
    key_pos = jnp.arange(s_len)

    def block_fn(args):
        qblk, i = args
        dist = (i * BLOCK + jnp.arange(BLOCK))[:, None] - key_pos[None, :]
        logits = jnp.einsum('bhjqd,bhjkd->bhjqk', qblk, kt).astype(jnp.float32) * (d ** -0.5)
        logits = logits - slopes[:, None, None, None] * dist.astype(jnp.float32)
        logits = jnp.where(dist >= 0, logits, -jnp.inf)
        a = jax.nn.softmax(logits, axis=-1)
        w = a[:, :, 0] - lam * a[:, :, 1]
        return jnp.einsum('bhqk,bhke->bhqe', w.astype(vt.dtype), vt)

    o = lax.map(block_fn, (qb, jnp.arange(nb)))
    o32 = o.astype(jnp.float32)
    o32 = o32 * lax.rsqrt(jnp.mean(jnp.square(o32), axis=-1, keepdims=True) + LN_EPS)
    o32 = o32 * subln_g.astype(jnp.float32) * (1.0 - lambda_init)
    return o32.astype(q.dtype).transpose(1, 0, 3, 2, 4).reshape(b, s_len, h * 2 * d)


def mixer_swa_sinks(q, k, v, sinks, slopes):
    b, s_len, _, d = q.shape
    qg = q.reshape(b, s_len, KV_HEADS_C, GQA_GROUP_C, d).transpose(0, 2, 3, 1, 4)
    kt = k.transpose(0, 2, 1, 3)
    vt = v.transpose(0, 2, 1, 3)
    logits = banded_logits(qg, kt, WINDOW_C - 1, 1, slopes.reshape(KV_HEADS_C, GQA_GROUP_C))
    sink = sinks.astype(jnp.float32).reshape(1, KV_HEADS_C, GQA_GROUP_C, 1, 1, 1)
    m = jnp.maximum(jnp.max(logits, axis=-1, keepdims=True), sink)
    p = jnp.exp(logits - m)
    denom = jnp.sum(p, axis=-1, keepdims=True) + jnp.exp(sink - m)
    o = banded_values(p / denom, vt)
    return o.transpose(0, 3, 1, 2, 4).reshape(b, s_len, HEADS_C * d)


def mixer_sublayer(h, w_in, w_gate, b_gate, w_branch, w_o, lam_vecs, subln_g, sinks, lambda_init):
    b, s_len, _ = h.shape
    proj = h @ w_in
    pts = [int(p) for p in np.cumsum(IN_SPLITS)[:-1]]
    qa, ka, va, qb, kb, vb, qc, kc, vc = jnp.split(proj, pts, axis=-1)
    shp_a = (b, s_len, N_GROUPS_A, HEADS_PER_GROUP_A, HEAD_DIM_A)
    y_a = mixer_dilated(qa.reshape(shp_a), ka.reshape(shp_a), va.reshape(shp_a),
                        alibi_slopes(N_GROUPS_A * HEADS_PER_GROUP_A).reshape(N_GROUPS_A, HEADS_PER_GROUP_A))
    y_b = mixer_diff(qb.reshape(b, s_len, HEADS_B, 2, HEAD_DIM_B), kb.reshape(b, s_len, HEADS_B, 2, HEAD_DIM_B),
                     vb.reshape(b, s_len, HEADS_B, 2 * HEAD_DIM_B), lam_vecs, subln_g, lambda_init,
                     alibi_slopes(HEADS_B))
    y_c = mixer_swa_sinks(qc.reshape(b, s_len, HEADS_C, HEAD_DIM_C), kc.reshape(b, s_len, KV_HEADS_C, HEAD_DIM_C),
                          vc.reshape(b, s_len, KV_HEADS_C, HEAD_DIM_C), sinks, alibi_slopes(HEADS_C))
    merged = None
    for n, y in enumerate((y_a, y_b, y_c)):
        gate = jax.nn.sigmoid((h @ w_gate[n] + b_gate[n]).astype(jnp.float32)).astype(h.dtype)
        term = gate * (y @ w_branch[n])
        merged = term if merged is None else merged + term
    return merged @ w_o


def moe_sublayer(h, w_rg, b_rg, w_rf, b_rf, w1, w3, w2):
    b, s_len, dm = h.shape
    t = h.reshape(-1, dm)
    p_group = jax.nn.softmax((t @ w_rg + b_rg).astype(jnp.float32), axis=-1)
    g_val, g_idx = lax.top_k(p_group, 1)
    g_onehot = jax.nn.one_hot(g_idx[:, 0], N_EXPERT_GROUPS, dtype=jnp.float32)
    fine = (t @ w_rf + b_rf).astype(jnp.float32).reshape(-1, N_EXPERT_GROUPS, EXPERTS_PER_GROUP)
    fine_sel = jnp.sum(fine * g_onehot[:, :, None], axis=1)
    e_val, e_idx = lax.top_k(fine_sel, TOP_K_EXPERTS)
    e_w = jax.nn.softmax(e_val, axis=-1) * g_val
    within = jnp.sum(jax.nn.one_hot(e_idx, EXPERTS_PER_GROUP, dtype=jnp.float32) * e_w[..., None], axis=1)
    gate = (g_onehot[:, :, None] * within[:, None, :]).reshape(-1, N_EXPERTS).astype(t.dtype)
    out = jnp.zeros_like(t)
    for e in range(N_EXPERTS):
        he = jax.nn.silu(t @ w1[e]) * (t @ w3[e])
        out = out + gate[:, e:e + 1] * (he @ w2[e])
    return out.reshape(b, s_len, dm)


def setup_inputs(seed: int = 0) -> dict:
    key = jax.random.key(seed)
    ks = jax.random.split(key, 24)
    f32 = jnp.float32

    def nrm(k, shape, scale):
        return jax.random.normal(k, shape, f32) * scale

    d = D_MODEL
    return {
        'x': nrm(ks[0], (BATCH, SEQ, d), 1.0),
        'c': nrm(ks[1], (BATCH, d), 1.0),
        'w_ada': nrm(ks[2], (DEPTH, d, 6 * d), d ** -0.5),
        'b_ada': nrm(ks[3], (DEPTH, 6 * d), 0.02),
        'w_in': nrm(ks[4], (DEPTH, d, IN_COLS), d ** -0.5),
        'w_gate': nrm(ks[5], (DEPTH, N_BRANCH, d, d), d ** -0.5),
        'b_gate': nrm(ks[6], (DEPTH, N_BRANCH, d), 0.02),
        'w_branch': nrm(ks[7], (DEPTH, N_BRANCH, BRANCH_WIDTH, d), BRANCH_WIDTH ** -0.5),
        'w_o': nrm(ks[8], (DEPTH, d, d), d ** -0.5 * DEEPNORM_BETA),
        'lam_vecs': nrm(ks[9], (DEPTH, 4, HEAD_DIM_B), 0.1),
        'subln_g': 1.0 + nrm(ks[10], (DEPTH, 2 * HEAD_DIM_B), 0.02),
        'sinks': nrm(ks[11], (DEPTH, HEADS_C), 1.0),
        'ln_g': 1.0 + nrm(ks[12], (DEPTH, 2, d), 0.02),
        'ln_b': nrm(ks[13], (DEPTH, 2, d), 0.02),
        'w_rg': nrm(ks[14], (DEPTH, d, N_EXPERT_GROUPS), d ** -0.5),
        'b_rg': nrm(ks[15], (DEPTH, N_EXPERT_GROUPS), 0.01),
        'w_rf': nrm(ks[16], (DEPTH, d, N_EXPERTS), d ** -0.5),
        'b_rf': nrm(ks[17], (DEPTH, N_EXPERTS), 0.01),
        'w1': nrm(ks[18], (DEPTH, N_EXPERTS, d, D_EXPERT), d ** -0.5),
        'w3': nrm(ks[19], (DEPTH, N_EXPERTS, d, D_EXPERT), d ** -0.5),
        'w2': nrm(ks[20], (DEPTH, N_EXPERTS, D_EXPERT, d), D_EXPERT ** -0.5 * DEEPNORM_BETA),
    }


def reference(x, c, w_ada, b_ada, w_in, w_gate, b_gate, w_branch, w_o, lam_vecs, subln_g, sinks,
              ln_g, ln_b, w_rg, b_rg, w_rf, b_rf, w1, w3, w2):
    cond = jax.nn.silu(c)
    for l in range(DEPTH):
        lambda_init = 0.8 - 0.6 * math.exp(-0.3 * l)
        ada = cond @ w_ada[l] + b_ada[l]
        sh1, sc1, g1, sh2, sc2, g2 = [a[:, None, :] for a in jnp.split(ada, 6, axis=-1)]
        h = x * (1.0 + sc1) + sh1
        mix = mixer_sublayer(h, w_in[l], w_gate[l], b_gate[l], w_branch[l], w_o[l],
                             lam_vecs[l], subln_g[l], sinks[l], lambda_init)
        x = layer_norm(DEEPNORM_ALPHA * x + g1 * mix, ln_g[l, 0], ln_b[l, 0])
        h = x * (1.0 + sc2) + sh2
        ffn = moe_sublayer(h, w_rg[l], b_rg[l], w_rf[l], b_rf[l], w1[l], w3[l], w2[l])
        x = layer_norm(DEEPNORM_ALPHA * x + g2 * ffn, ln_g[l, 1], ln_b[l, 1])
    return x
```

```python
import functools

import numpy as np
import jax
import jax.numpy as jnp
from jax import lax
from jax.experimental import pallas as pl
from jax.experimental.pallas import tpu as pltpu

F32 = jnp.float32
BF16 = jnp.bfloat16
HIGHEST = lax.Precision.HIGHEST

BLOCK = 128
DILATED_GROUPS = ((128, 1), (512, 4), (2048, 16))
N_GROUPS_A = 3
HEADS_A = 4
HEAD_DIM_A = 128
GROUP_WIDTH_A = HEADS_A * HEAD_DIM_A
QKV_A = N_GROUPS_A * GROUP_WIDTH_A
HEADS_B = 4
HEAD_DIM_B = 64
HEADS_C = 8
KV_HEADS_C = 2
GQA_GROUP_C = HEADS_C // KV_HEADS_C
HEAD_DIM_C = 64
WINDOW_C = 128
BRANCH_WIDTH = 512
N_EXPERT_GROUPS = 4
EXPERTS_PER_GROUP = 8
N_EXPERTS = N_EXPERT_GROUPS * EXPERTS_PER_GROUP
LN_EPS = 1e-5
NEG = -1e30

LANES = 128
ROUTE_LANE0 = N_EXPERT_GROUPS
VMEM_LIMIT = 56 * 1024 * 1024

EXPERT_TILE = 512


def _alibi_slopes(n):
    return [float(v) for v in 2.0 ** (-8.0 * np.arange(1, n + 1, dtype=np.float32) / n)]


def _params(*sem):
    return pltpu.CompilerParams(dimension_semantics=sem, vmem_limit_bytes=VMEM_LIMIT)


def _ada_kernel(c_ref, w_ref, b_ref, o_ref):
    c = c_ref[...]
    cond = c * jax.nn.sigmoid(c)
    o_ref[0] = jnp.dot(cond, w_ref[0], preferred_element_type=F32, precision=HIGHEST) + b_ref[0]


def _ada(c, w_ada, b_ada):
    depth, d, n = w_ada.shape
    bsz = c.shape[0]
    tn = 1024
    return pl.pallas_call(
        _ada_kernel,
        grid=(depth, n // tn),
        in_specs=[pl.BlockSpec((bsz, d), lambda l, j: (0, 0)),
                  pl.BlockSpec((1, d, tn), lambda l, j: (l, 0, j)),
                  pl.BlockSpec((1, 1, tn), lambda l, j: (l, 0, j))],
        out_specs=pl.BlockSpec((1, bsz, tn), lambda l, j: (l, 0, j)),
        out_shape=jax.ShapeDtypeStruct((depth, bsz, n), F32),
        compiler_params=_params("arbitrary", "arbitrary"),
        name="ada",
    )(c, w_ada, b_ada.reshape(depth, 1, n))


def _modulate_kernel(x_ref, sh_ref, sc_ref, o_ref):
    o_ref[...] = (x_ref[...] * (1.0 + sc_ref[0]) + sh_ref[0]).astype(o_ref.dtype)


def _modulate(x2, ada_l, seq, tm=512):
    t, d = x2.shape
    per_b = seq // tm
    return pl.pallas_call(
        _modulate_kernel,
        grid=(t // tm,),
        in_specs=[pl.BlockSpec((tm, d), lambda i: (i, 0)),
                  pl.BlockSpec((1, 1, d), lambda i: (i // per_b, 0, 0)),
                  pl.BlockSpec((1, 1, d), lambda i: (i // per_b, 0, 1))],
        out_specs=pl.BlockSpec((tm, d), lambda i: (i, 0)),
        out_shape=jax.ShapeDtypeStruct((t, d), BF16),
        compiler_params=_params("arbitrary"),
        name="modulate",
    )(x2, ada_l, ada_l)


def _mm_kernel(x_ref, w_ref, o_ref):
    o_ref[...] = jnp.dot(x_ref[...], w_ref[...], preferred_element_type=F32).astype(o_ref.dtype)


def _matmul(x2, w, tm=1024, tn=768):
    t, d = x2.shape
    n = w.shape[1]
    tm = min(tm, t)
    return pl.pallas_call(
        _mm_kernel,
        grid=(t // tm, n // tn),
        in_specs=[pl.BlockSpec((tm, d), lambda i, j: (i, 0)),
                  pl.BlockSpec((d, tn), lambda i, j: (0, j))],
        out_specs=pl.BlockSpec((tm, tn), lambda i, j: (i, j)),
        out_shape=jax.ShapeDtypeStruct((t, n), BF16),
        compiler_params=_params("arbitrary", "arbitrary"),
        name="proj",
    )(x2, w)


def _gate_kernel(x_ref, w_ref, b_ref, o_ref):
    acc = jnp.dot(x_ref[...], w_ref[0], preferred_element_type=F32) + b_ref[0]
    o_ref[...] = jax.nn.sigmoid(acc).astype(o_ref.dtype)


def _gates(x2, w_gate, b_gate, tm=1024, tn=1024):
    t, d = x2.shape
    nbr, _, n = w_gate.shape
    tm = min(tm, t)
    per = n // tn
    return pl.pallas_call(
        _gate_kernel,
        grid=(t // tm, nbr * per),
        in_specs=[pl.BlockSpec((tm, d), lambda i, j: (i, 0)),
                  pl.BlockSpec((1, d, tn), lambda i, j: (j // per, 0, j % per)),
                  pl.BlockSpec((1, 1, tn), lambda i, j: (j // per, 0, j % per))],
        out_specs=pl.BlockSpec((tm, tn), lambda i, j: (i, j)),
        out_shape=jax.ShapeDtypeStruct((t, nbr * n), BF16),
        compiler_params=_params("arbitrary", "arbitrary"),
        name="gates",
    )(x2, w_gate, b_gate)


def _attn_a_kernel(q_ref, kp_ref, kc_ref, vp_ref, vc_ref, o_ref, lse_ref, kbuf, vbuf, *, slopes, dist_unit, nb):
    n = pl.program_id(1)
    kbuf[0:BLOCK, :] = kp_ref[0]
    kbuf[BLOCK:, :] = kc_ref[0]
    vbuf[0:BLOCK, :] = vp_ref[0]
    vbuf[BLOCK:, :] = vc_ref[0]
    row = lax.broadcasted_iota(jnp.int32, (BLOCK, 2 * BLOCK), 0)
    col = lax.broadcasted_iota(jnp.int32, (BLOCK, 2 * BLOCK), 1)
    dist = row + BLOCK - col
    band = (dist >= 0) & (dist <= BLOCK)
    first_lim = jnp.where(n > 0, 0, BLOCK)
    band_first = band & (col >= first_lim)
    distf = dist.astype(F32) * float(dist_unit)
    scale = HEAD_DIM_A ** -0.5
    for h in range(HEADS_A):
        hs = slice(h * HEAD_DIM_A, (h + 1) * HEAD_DIM_A)
        bias = -slopes[h] * distf
        for i in range(nb // BLOCK):
            rows = slice(i * BLOCK, (i + 1) * BLOCK)
            q = q_ref[0, rows, hs]
            k = kbuf[i * BLOCK:(i + 2) * BLOCK, hs]
            v = vbuf[i * BLOCK:(i + 2) * BLOCK, hs]
            s = lax.dot_general(q, k, (((1,), (1,)), ((), ())), preferred_element_type=F32) * scale + bias
            s = jnp.where(band_first if i == 0 else band, s, NEG)
            m = jnp.max(s, axis=-1, keepdims=True)
            p = jnp.exp(s - m)
            l = jnp.sum(p, axis=-1, keepdims=True)
            o = jnp.dot(p.astype(BF16), v, preferred_element_type=F32) / l
            o_ref[0, rows, hs] = o.astype(o_ref.dtype)
            lse_ref[0, rows, h:h + 1] = m + jnp.log(l)


def _attn_a_group(qkv, cols, slopes, dist_unit):
    nseq, l, _ = qkv.shape
    nb = min(512, l)
    sub = nb // BLOCK
    cq, ck, cv = cols
    w = GROUP_WIDTH_A
    kern = functools.partial(_attn_a_kernel, slopes=slopes, dist_unit=dist_unit, nb=nb)
    prev = lambda b, n: jnp.maximum(n * sub - 1, 0)
    return pl.pallas_call(
        kern,
        grid=(nseq, l // nb),
        in_specs=[pl.BlockSpec((1, nb, w), lambda b, n: (b, n, cq)),
                  pl.BlockSpec((1, BLOCK, w), lambda b, n: (b, prev(b, n), ck)),
                  pl.BlockSpec((1, nb, w), lambda b, n: (b, n, ck)),
                  pl.BlockSpec((1, BLOCK, w), lambda b, n: (b, prev(b, n), cv)),
                  pl.BlockSpec((1, nb, w), lambda b, n: (b, n, cv))],
        out_specs=[pl.BlockSpec((1, nb, w), lambda b, n: (b, n, 0)),
                   pl.BlockSpec((1, nb, HEADS_A), lambda b, n: (b, n, 0))],
        out_shape=[jax.ShapeDtypeStruct((nseq, l, w), BF16),
                   jax.ShapeDtypeStruct((nseq, l, HEADS_A), F32)],
        scratch_shapes=[pltpu.VMEM((BLOCK + nb, w), BF16), pltpu.VMEM((BLOCK + nb, w), BF16)],
        compiler_params=_params("arbitrary", "arbitrary"),
        name="attn_a",
    )(qkv, qkv, qkv, qkv, qkv)


def _mixer_a(proj_a):
    b, s, _ = proj_a.shape
    slopes = _alibi_slopes(N_GROUPS_A * HEADS_A)
    outs, lses = [], []
    for g, (window, r) in enumerate(DILATED_GROUPS):
        sl = slopes[g * HEADS_A:(g + 1) * HEADS_A]
        assert window // r == BLOCK
        if r == 1:
            o, lse = _attn_a_group(proj_a, (g, 3 + g, 6 + g), sl, 1)
        else:
            parts = [proj_a[:, :, p * QKV_A + g * GROUP_WIDTH_A: p * QKV_A + (g + 1) * GROUP_WIDTH_A] for p in range(3)]
            qkv = jnp.concatenate(parts, axis=-1)
            qkv = qkv.reshape(b, s // r, r, 3 * GROUP_WIDTH_A).transpose(0, 2, 1, 3).reshape(b * r, s // r, 3 * GROUP_WIDTH_A)
            o, lse = _attn_a_group(qkv, (0, 1, 2), sl, r)
            o = o.reshape(b, r, s // r, GROUP_WIDTH_A).transpose(0, 2, 1, 3).reshape(b, s, GROUP_WIDTH_A)
            lse = lse.reshape(b, r, s // r, HEADS_A).transpose(0, 2, 1, 3).reshape(b, s, HEADS_A)
        outs.append(o)
        lses.append(lse)
    return outs, lses


def _attn_b_kernel(q_ref, k_ref, v_ref, lv_ref, g_ref, o_ref, *, slopes, lambda_init, tq):
    h = pl.program_id(1)
    qi = pl.program_id(2)
    d = HEAD_DIM_B
    slope = jnp.float32(slopes[0])
    for j in range(1, HEADS_B):
        slope = jnp.where(h == j, jnp.float32(slopes[j]), slope)
    q = q_ref[0]
    qs = [(q[:, j * d:(j + 1) * d].astype(F32) * (d ** -0.5)).astype(BF16) for j in range(2)]
    row = lax.broadcasted_iota(jnp.int32, (tq, tq), 0)
    col = lax.broadcasted_iota(jnp.int32, (tq, tq), 1)
    colf = lax.broadcasted_iota(jnp.int32, (1, tq), 1).astype(F32)

    def block(kb, carry, masked):
        start = pl.multiple_of(kb * tq, tq)
        k = k_ref[0, pl.ds(start, tq), :]
        v = v_ref[0, pl.ds(start, tq), :]
        bias = slope * (colf + ((kb - qi) * tq).astype(F32))
        new = []
        for j in range(2):
            m, l, acc = carry[j]
            s = lax.dot_general(qs[j], k[:, j * d:(j + 1) * d], (((1,), (1,)), ((), ())),
                                preferred_element_type=F32) + bias
            if masked:
                s = jnp.where(col <= row, s, NEG)
            m_new = jnp.maximum(m, jnp.max(s, axis=-1, keepdims=True))
            alpha = jnp.exp(m - m_new)
            p = jnp.exp(s - m_new)
            l = alpha * l + jnp.sum(p, axis=-1, keepdims=True)
            acc = alpha * acc + jnp.dot(p.astype(BF16), v, preferred_element_type=F32)
            new.append((m_new, l, acc))
        return tuple(new)

    init = tuple((jnp.full((tq, 1), NEG, F32), jnp.zeros((tq, 1), F32), jnp.zeros((tq, 2 * d), F32)) for _ in range(2))
    carry = lax.fori_loop(0, qi, lambda kb, c: block(kb, c, False), init)
    (_, l0, a0), (_, l1, a1) = block(qi, carry, True)
    lv = lv_ref[...]
    lam = (jnp.exp(jnp.sum(lv[0:1] * lv[1:2], keepdims=True)) - jnp.exp(jnp.sum(lv[2:3] * lv[3:4], keepdims=True))
           + lambda_init)
    o = a0 / l0 - lam * (a1 / l1)
    o = o * lax.rsqrt(jnp.mean(o * o, axis=-1, keepdims=True) + LN_EPS)
    o = o * g_ref[...] * (1.0 - lambda_init)
    o_ref[0] = o.astype(o_ref.dtype)


def _mixer_b(proj_bc, lam_vecs, subln_g, lambda_init, tq=256):
    b, s, _ = proj_bc.shape
    w = 2 * HEAD_DIM_B
    kern = functools.partial(_attn_b_kernel, slopes=_alibi_slopes(HEADS_B), lambda_init=lambda_init, tq=tq)
    return pl.pallas_call(
        kern,
        grid=(b, HEADS_B, s // tq),
        in_specs=[pl.BlockSpec((1, tq, w), lambda bi, h, i: (bi, i, h)),
                  pl.BlockSpec((1, s, w), lambda bi, h, i: (bi, 0, HEADS_B + h)),
                  pl.BlockSpec((1, s, w), lambda bi, h, i: (bi, 0, 2 * HEADS_B + h)),
                  pl.BlockSpec(lam_vecs.shape, lambda bi, h, i: (0, 0)),
                  pl.BlockSpec((1, w), lambda bi, h, i: (0, 0))],
        out_specs=pl.BlockSpec((1, tq, w), lambda bi, h, i: (bi, i, h)),
        out_shape=jax.ShapeDtypeStruct((b, s, HEADS_B * w), BF16),
        compiler_params=_params("arbitrary", "arbitrary", "arbitrary"),
        name="attn_b",
    )(proj_bc, proj_bc, proj_bc, lam_vecs, subln_g.reshape(1, w))


def _attn_c_kernel(sink_ref, q_ref, kp_ref, kc_ref, vp_ref, vc_ref, o_ref, kbuf, vbuf, *, slopes, nb):
    n = pl.program_id(1)
    kbuf[0:BLOCK, :] = kp_ref[0]
    kbuf[BLOCK:, :] = kc_ref[0]
    vbuf[0:BLOCK, :] = vp_ref[0]
    vbuf[BLOCK:, :] = vc_ref[0]
    row = lax.broadcasted_iota(jnp.int32, (BLOCK, 2 * BLOCK), 0)
    col = lax.broadcasted_iota(jnp.int32, (BLOCK, 2 * BLOCK), 1)
    dist = row + BLOCK - col
    band = (dist >= 0) & (dist <= WINDOW_C - 1)
    first_lim = jnp.where(n > 0, 0, BLOCK)
    band_first = band & (col >= first_lim)
    distf = dist.astype(F32)
    d = HEAD_DIM_C
    scale = d ** -0.5
    for h in range(HEADS_C):
        kv = h // GQA_GROUP_C
        bias = -slopes[h] * distf
        sink = sink_ref[h]
        for i in range(nb // BLOCK):
            rows = slice(i * BLOCK, (i + 1) * BLOCK)
            q = q_ref[0, rows, h * d:(h + 1) * d]
            k = kbuf[i * BLOCK:(i + 2) * BLOCK, kv * d:(kv + 1) * d]
            v = vbuf[i * BLOCK:(i + 2) * BLOCK, kv * d:(kv + 1) * d]
            s = lax.dot_general(q, k, (((1,), (1,)), ((), ())), preferred_element_type=F32) * scale + bias
            s = jnp.where(band_first if i == 0 else band, s, NEG)
            m = jnp.maximum(jnp.max(s, axis=-1, keepdims=True), sink)
            p = jnp.exp(s - m)
            denom = jnp.sum(p, axis=-1, keepdims=True) + jnp.exp(sink - m)
            o = jnp.dot(p.astype(BF16), v, preferred_element_type=F32) / denom
            o_ref[0, rows, h * d:(h + 1) * d] = o.astype(o_ref.dtype)


def _mixer_c(proj_bc, sinks):
    b, s, _ = proj_bc.shape
    nb = min(512, s)
    sub = nb // BLOCK
    wq = HEADS_C * HEAD_DIM_C
    wkv = KV_HEADS_C * HEAD_DIM_C
    cq, ck, cv = 1536 // wq, 2048 // wkv, 2176 // wkv
    kern = functools.partial(_attn_c_kernel, slopes=_alibi_slopes(HEADS_C), nb=nb)
    prev = lambda n: jnp.maximum(n * sub - 1, 0)
    return pl.pallas_call(
        kern,
        grid=(b, s // nb),
        in_specs=[pl.BlockSpec(memory_space=pltpu.SMEM),
                  pl.BlockSpec((1, nb, wq), lambda bi, n: (bi, n, cq)),
                  pl.BlockSpec((1, BLOCK, wkv), lambda bi, n: (bi, prev(n), ck)),
                  pl.BlockSpec((1, nb, wkv), lambda bi, n: (bi, n, ck)),
                  pl.BlockSpec((1, BLOCK, wkv), lambda bi, n: (bi, prev(n), cv)),
                  pl.BlockSpec((1, nb, wkv), lambda bi, n: (bi, n, cv))],
        out_specs=pl.BlockSpec((1, nb, wq), lambda bi, n: (bi, n, 0)),
        out_shape=jax.ShapeDtypeStruct((b, s, wq), BF16),
        scratch_shapes=[pltpu.VMEM((BLOCK + nb, wkv), BF16), pltpu.VMEM((BLOCK + nb, wkv), BF16)],
        compiler_params=_params("arbitrary", "arbitrary"),
        name="attn_c",
    )(sinks, proj_bc, proj_bc, proj_bc, proj_bc, proj_bc)


def _merge_kernel(o1_ref, o2_ref, o3_ref, l1_ref, l2_ref, l3_ref, yb_ref, yc_ref, g_ref, wb_ref, out_ref):
    la, lb, lc = l1_ref[...], l2_ref[...], l3_ref[...]
    m = jnp.maximum(jnp.maximum(la, lb), lc)
    ea, eb, ec = jnp.exp(la - m), jnp.exp(lb - m), jnp.exp(lc - m)
    den = ea + eb + ec
    wa, wb, wc = ea / den, eb / den, ec / den
    parts = []
    for h in range(HEADS_A):
        hs = slice(h * HEAD_DIM_A, (h + 1) * HEAD_DIM_A)
        y = (wa[:, h:h + 1] * o1_ref[:, hs].astype(F32) + wb[:, h:h + 1] * o2_ref[:, hs].astype(F32)
             + wc[:, h:h + 1] * o3_ref[:, hs].astype(F32))
        parts.append(y.astype(BF16))
    ya = jnp.concatenate(parts, axis=1)
    dm = out_ref.shape[1]
    acc = g_ref[:, 0:dm].astype(F32) * jnp.dot(ya, wb_ref[0], preferred_element_type=F32)
    acc = acc + g_ref[:, dm:2 * dm].astype(F32) * jnp.dot(yb_ref[...], wb_ref[1], preferred_element_type=F32)
    acc = acc + g_ref[:, 2 * dm:3 * dm].astype(F32) * jnp.dot(yc_ref[...], wb_ref[2], preferred_element_type=F32)
    out_ref[...] = acc.astype(out_ref.dtype)


def _merge(outs_a, lses_a, y_b, y_c, gates, w_branch, tm=512):
    t = y_b.shape[0]
    dm = w_branch.shape[2]
    wide = lambda w: pl.BlockSpec((tm, w), lambda i: (i, 0))
    return pl.pallas_call(
        _merge_kernel,
        grid=(t // tm,),
        in_specs=[wide(BRANCH_WIDTH)] * 3 + [wide(HEADS_A)] * 3 + [wide(BRANCH_WIDTH)] * 2 + [wide(3 * dm)]
                 + [pl.BlockSpec(w_branch.shape, lambda i: (0, 0, 0))],
        out_specs=wide(dm),
        out_shape=jax.ShapeDtypeStruct((t, dm), BF16),
        compiler_params=_params("arbitrary"),
        name="merge",
    )(*outs_a, *lses_a, y_b, y_c, gates, w_branch)


def _layer_norm(z, g, b):
    mu = jnp.mean(z, axis=-1, keepdims=True)
    zc = z - mu
    var = jnp.mean(zc * zc, axis=-1, keepdims=True)
    return zc * lax.rsqrt(var + LN_EPS) * g + b


def _wo_ln_route_kernel(a_ref, wo_ref, x_ref, g1_ref, sh_ref, sc_ref, lng_ref, lnb_ref, wr_ref, br_ref,
                        xo_ref, h_ref, route_ref, cnt_ref, run_ref, *, alpha, tm):
    @pl.when(pl.program_id(0) == 0)
    def _():
        run_ref[...] = jnp.zeros_like(run_ref)

    mix = jnp.dot(a_ref[...], wo_ref[...], preferred_element_type=F32)
    xn = _layer_norm(alpha * x_ref[...] + g1_ref[0] * mix, lng_ref[...], lnb_ref[...])
    xo_ref[...] = xn
    h = xn * (1.0 + sc_ref[0]) + sh_ref[0]
    h_ref[...] = h

    logits = jnp.dot(h, wr_ref[...], preferred_element_type=F32, precision=HIGHEST) + br_ref[...]
    lane = lax.broadcasted_iota(jnp.int32, (tm, LANES), 1).astype(F32)
    far = float(4 * LANES)
    gl = jnp.where(lane < N_EXPERT_GROUPS, logits, NEG)
    gm = jnp.max(gl, axis=-1, keepdims=True)
    g_val = 1.0 / jnp.sum(jnp.exp(gl - gm), axis=-1, keepdims=True)
    g_idx = jnp.min(jnp.where(gl == gm, lane, far), axis=-1, keepdims=True)
    lo = ROUTE_LANE0 + EXPERTS_PER_GROUP * g_idx
    fl = jnp.where((lane >= lo) & (lane < lo + EXPERTS_PER_GROUP), logits, NEG)
    m1 = jnp.max(fl, axis=-1, keepdims=True)
    i1 = jnp.min(jnp.where(fl == m1, lane, far), axis=-1, keepdims=True)
    fl2 = jnp.where(lane == i1, NEG, fl)
    m2 = jnp.max(fl2, axis=-1, keepdims=True)
    i2 = jnp.min(jnp.where(fl2 == m2, lane, far), axis=-1, keepdims=True)
    e = jnp.exp(m2 - m1)
    w1 = g_val / (1.0 + e)
    w2 = g_val * e / (1.0 + e)

    oh1 = lane == i1
    oh2 = lane == i2
    c = jnp.where(oh1 | oh2, 1.0, 0.0)
    r_i = lax.broadcasted_iota(jnp.int32, (tm, tm), 0)
    c_i = lax.broadcasted_iota(jnp.int32, (tm, tm), 1)
    before = jnp.where(c_i < r_i, 1.0, 0.0).astype(BF16)
    tot = jnp.dot(before, c.astype(BF16), preferred_element_type=F32) + run_ref[...]
    rank1 = jnp.sum(jnp.where(oh1, tot, 0.0), axis=-1, keepdims=True)
    rank2 = jnp.sum(jnp.where(oh2, tot, 0.0), axis=-1, keepdims=True)
    run_ref[...] = run_ref[...] + jnp.sum(c, axis=0, keepdims=True)
    cnt_ref[...] = run_ref[...]

    l8 = lax.broadcasted_iota(jnp.int32, (tm, 8), 1)
    e1 = i1 - ROUTE_LANE0
    e2 = i2 - ROUTE_LANE0
    route = jnp.where(l8 == 0, w1, jnp.where(l8 == 1, w2, jnp.where(l8 == 2, e1, jnp.where(l8 == 3, e2,
            jnp.where(l8 == 4, rank1, rank2)))))
    route_ref[...] = route


def _wo_ln_route(merged, w_o, x2, ada_l, ln_g, ln_b, w_r, b_r, seq, alpha, tm=256):
    t, d = x2.shape
    per_b = seq // tm
    vec = lambda k: pl.BlockSpec((1, 1, d), lambda i: (i // per_b, 0, k))
    row = lambda w: pl.BlockSpec((tm, w), lambda i: (i, 0))
    full = lambda a: pl.BlockSpec(a.shape, lambda i: (0,) * a.ndim)
    kern = functools.partial(_wo_ln_route_kernel, alpha=alpha, tm=tm)
    return pl.pallas_call(
        kern,
        grid=(t // tm,),
        in_specs=[row(d), full(w_o), row(d), vec(2), vec(3), vec(4), full(ln_g), full(ln_b), full(w_r), full(b_r)],
        out_specs=[row(d), row(d), row(8), pl.BlockSpec((1, LANES), lambda i: (0, 0))],
        out_shape=[jax.ShapeDtypeStruct((t, d), F32), jax.ShapeDtypeStruct((t, d), F32),
                   jax.ShapeDtypeStruct((t, 8), F32), jax.ShapeDtypeStruct((1, LANES), F32)],
        scratch_shapes=[pltpu.VMEM((1, LANES), F32)],
        compiler_params=_params("arbitrary"),
        name="wo_ln_route",
    )(merged, w_o, x2, ada_l, ada_l, ada_l, ln_g, ln_b, w_r, b_r)


def _row_copy(src, i, dst, j, sem):
    return pltpu.make_async_copy(src.at[pl.ds(i, 1), :], dst.at[pl.ds(j, 1), :], sem)


def _dispatch_kernel(pos_ref, h_hbm, xs_hbm, sem, *, tc):
    base = pl.program_id(0) * tc

    def issue(i, c):
        _row_copy(h_hbm, base + i, xs_hbm, pos_ref[2 * i], sem).start()
        _row_copy(h_hbm, base + i, xs_hbm, pos_ref[2 * i + 1], sem).start()
        return c

    lax.fori_loop(0, tc, issue, 0)

    def drain(i, c):
        _row_copy(h_hbm, 0, xs_hbm, 0, sem).wait()
        return c

    lax.fori_loop(0, 2 * tc, drain, 0)


def _dispatch(h2, pos, tc=512):
    t, d = h2.shape
    return pl.pallas_call(
        functools.partial(_dispatch_kernel, tc=tc),
        grid=(t // tc,),
        in_specs=[pl.BlockSpec((2 * tc,), lambda i: (i,), memory_space=pltpu.SMEM),
                  pl.BlockSpec(memory_space=pl.ANY)],
        out_specs=pl.BlockSpec(memory_space=pl.ANY),
        out_shape=jax.ShapeDtypeStruct((2 * t, d), F32),
        scratch_shapes=[pltpu.SemaphoreType.DMA(())],
        compiler_params=_params("arbitrary"),
        name="dispatch",
    )(pos, h2)


def _expert_kernel(it_ref, ie_ref, lo_ref, hi_ref, first_ref, nu_ref, x_ref, w1_ref, w3_ref, w2_ref, y_ref, *, tile):
    i = pl.program_id(0)

    @pl.when(i < nu_ref[0])
    def _():
        x = x_ref[...].astype(BF16)
        a = jnp.dot(x, w1_ref[0], preferred_element_type=F32)
        b = jnp.dot(x, w3_ref[0], preferred_element_type=F32)
        he = (a * jax.nn.sigmoid(a) * b).astype(BF16)
        y = jnp.dot(he, w2_ref[0], preferred_element_type=F32)
        row = it_ref[i] * tile + lax.broadcasted_iota(jnp.int32, (tile, 1), 0)
        mine = (row >= lo_ref[i]) & (row < hi_ref[i])

        @pl.when(first_ref[i] == 1)
        def _():
            y_ref[...] = jnp.where(mine, y, 0.0)

        @pl.when(first_ref[i] == 0)
        def _():
            y_ref[...] = jnp.where(mine, y, y_ref[...])


def _experts(xs, w1, w3, w2, plan):
    n_rows, d = xs.shape
    de = w1.shape[2]
    tile = EXPERT_TILE
    n_items = plan[0].shape[0]
    row_map = lambda i, it, ie, lo, hi, fi, nu: (it[i], 0)
    w_map = lambda i, it, ie, lo, hi, fi, nu: (ie[i], 0, 0)
    grid_spec = pltpu.PrefetchScalarGridSpec(
        num_scalar_prefetch=6,
        grid=(n_items,),
        in_specs=[pl.BlockSpec((tile, d), row_map),
                  pl.BlockSpec((1, d, de), w_map),
                  pl.BlockSpec((1, d, de), w_map),
                  pl.BlockSpec((1, de, d), w_map)],
        out_specs=pl.BlockSpec((tile, d), row_map),
    )
    return pl.pallas_call(
        functools.partial(_expert_kernel, tile=tile),
        grid_spec=grid_spec,
        out_shape=jax.ShapeDtypeStruct((n_rows, d), F32),
        compiler_params=_params("arbitrary"),
        name="experts",
    )(*plan, xs, w1, w3, w2)


def _combine_kernel(pos_ref, ys_hbm, route_ref, x_ref, g2_ref, lng_ref, lnb_ref, sh_ref, sc_ref,
                    xo_ref, h_ref, buf, sem, *, alpha, tc):
    def issue(i, c):
        _row_copy(ys_hbm, pos_ref[2 * i], buf.at[0], i, sem).start()
        _row_copy(ys_hbm, pos_ref[2 * i + 1], buf.at[1], i, sem).start()
        return c

    lax.fori_loop(0, tc, issue, 0)

    def drain(i, c):
        _row_copy(ys_hbm, 0, buf.at[0], 0, sem).wait()
        return c

    lax.fori_loop(0, 2 * tc, drain, 0)

    ffn = route_ref[:, 0:1] * buf[0] + route_ref[:, 1:2] * buf[1]
    xn = _layer_norm(alpha * x_ref[...] + g2_ref[0] * ffn, lng_ref[...], lnb_ref[...])
    xo_ref[...] = xn
    h_ref[...] = (xn * (1.0 + sc_ref[0]) + sh_ref[0]).astype(h_ref.dtype)


def _combine(ys, pos, route, x2, ada_l, ada_next, ln_g, ln_b, seq, alpha, tc=256):
    t, d = x2.shape
    per_b = seq // tc
    vec = lambda k: pl.BlockSpec((1, 1, d), lambda i: (i // per_b, 0, k))
    row = lambda w: pl.BlockSpec((tc, w), lambda i: (i, 0))
    full = lambda a: pl.BlockSpec(a.shape, lambda i: (0,) * a.ndim)
    kern = functools.partial(_combine_kernel, alpha=alpha, tc=tc)
    return pl.pallas_call(
        kern,
        grid=(t // tc,),
        in_specs=[pl.BlockSpec((2 * tc,), lambda i: (i,), memory_space=pltpu.SMEM),
                  pl.BlockSpec(memory_space=pl.ANY),
                  row(8), row(d), vec(5), full(ln_g), full(ln_b), vec(0), vec(1)],
        out_specs=[row(d), row(d)],
        out_shape=[jax.ShapeDtypeStruct((t, d), F32), jax.ShapeDtypeStruct((t, d), BF16)],
        scratch_shapes=[pltpu.VMEM((2, tc, d), F32), pltpu.SemaphoreType.DMA(())],
        compiler_params=_params("arbitrary"),
        name="combine",
    )(pos, ys, route, x2, ada_l, ln_g, ln_b, ada_next, ada_next)


def _dispatch_plan(route, counts, n_rows):
    tile = EXPERT_TILE
    n_items = n_rows // tile + N_EXPERTS
    cnt = counts[0, ROUTE_LANE0:ROUTE_LANE0 + N_EXPERTS].astype(jnp.int32)
    ends = jnp.cumsum(cnt)
    starts = ends - cnt
    e = route[:, 2:4].astype(jnp.int32)
    rank = route[:, 4:6].astype(jnp.int32)
    pos = (jnp.take(starts, e) + rank).reshape(-1)
    first_tile = starts // tile
    n_e = jnp.where(cnt > 0, (ends - 1) // tile - first_tile + 1, 0)
    item_ends = jnp.cumsum(n_e)
    n_used = item_ends[-1]
    k = jnp.minimum(jnp.arange(n_items, dtype=jnp.int32), n_used - 1)
    item_expert = jnp.minimum(jnp.searchsorted(item_ends, k, side="right"), N_EXPERTS - 1).astype(jnp.int32)
    item_tile = jnp.take(first_tile, item_expert) + k - jnp.take(item_ends - n_e, item_expert)
    item_first = jnp.concatenate([jnp.ones((1,), jnp.int32), (item_tile[1:] != item_tile[:-1]).astype(jnp.int32)])
    plan = (item_tile, item_expert, jnp.take(starts, item_expert), jnp.take(ends, item_expert), item_first,
            n_used.reshape(1))
    return pos, tuple(p.astype(jnp.int32) for p in plan)


def kernel(x, c, w_ada, b_ada, w_in, w_gate, b_gate, w_branch, w_o, lam_vecs, subln_g, sinks, ln_g, ln_b,
           w_rg, b_rg, w_rf, b_rf, w1, w3, w2):
    b, s, d = x.shape
    depth = w_ada.shape[0]
    t = b * s
    alpha = (2 * depth) ** 0.25
    assert s % (16 * BLOCK) == 0 and d % LANES == 0

    w_in16 = w_in.astype(BF16)
    w_gate16 = w_gate.astype(BF16)
    w_branch16 = w_branch.astype(BF16)
    w_o16 = w_o.astype(BF16)
    w1_16, w3_16, w2_16 = w1.astype(BF16), w3.astype(BF16), w2.astype(BF16)
    pad = LANES - N_EXPERT_GROUPS - N_EXPERTS
    w_r = jnp.concatenate([w_rg, w_rf, jnp.zeros((depth, d, pad), F32)], axis=-1)
    b_r = jnp.concatenate([b_rg, b_rf, jnp.zeros((depth, pad), F32)], axis=-1).reshape(depth, 1, LANES)

    ada = _ada(c, w_ada, b_ada).reshape(depth, b, 1, 6 * d)
    x2 = x.reshape(t, d)
    h = _modulate(x2, ada[0], s)

    for l in range(depth):
        lambda_init = 0.8 - 0.6 * float(np.exp(-0.3 * l))
        ada_l = ada[l]
        proj_a = _matmul(h, w_in16[l, :, :3 * QKV_A]).reshape(b, s, 3 * QKV_A)
        proj_bc = _matmul(h, w_in16[l, :, 3 * QKV_A:]).reshape(b, s, -1)
        gates = _gates(h, w_gate16[l], b_gate[l].reshape(3, 1, d))
        outs_a, lses_a = _mixer_a(proj_a)
        y_b = _mixer_b(proj_bc, lam_vecs[l], subln_g[l], lambda_init)
        y_c = _mixer_c(proj_bc, sinks[l])
        merged = _merge([o.reshape(t, -1) for o in outs_a], [v.reshape(t, -1) for v in lses_a],
                        y_b.reshape(t, -1), y_c.reshape(t, -1), gates, w_branch16[l])
        x2, h2, route, counts = _wo_ln_route(merged, w_o16[l], x2, ada_l, ln_g[l, 0:1], ln_b[l, 0:1],
                                             w_r[l], b_r[l], s, alpha)
        pos, plan = _dispatch_plan(route, counts, 2 * t)
        xs = _dispatch(h2, pos)
        ys = _experts(xs, w1_16[l], w3_16[l], w2_16[l], plan)
        ada_next = ada[min(l + 1, depth - 1)]
        x2, h = _combine(ys, pos, route, x2, ada_l, ada_next, ln_g[l, 1:2], ln_b[l, 1:2], s, alpha)
    return x2.reshape(b, s, d)
```

```python
import functools

import numpy as np
import jax
import jax.numpy as jnp
from jax import lax
from jax.experimental import pallas as pl
from jax.experimental.pallas import tpu as pltpu

F32 = jnp.float32
BF16 = jnp.bfloat16
HIGHEST = lax.Precision.HIGHEST

BLOCK = 128
DILATED_GROUPS = ((128, 1), (512, 4), (2048, 16))
N_GROUPS_A = 3
HEADS_A = 4
HEAD_DIM_A = 128
GROUP_WIDTH_A = HEADS_A * HEAD_DIM_A
QKV_A = N_GROUPS_A * GROUP_WIDTH_A
HEADS_B = 4
HEAD_DIM_B = 64
HEADS_C = 8
KV_HEADS_C = 2
GQA_GROUP_C = HEADS_C // KV_HEADS_C
HEAD_DIM_C = 64
WINDOW_C = 128
BRANCH_WIDTH = 512
N_EXPERT_GROUPS = 4
EXPERTS_PER_GROUP = 8
N_EXPERTS = N_EXPERT_GROUPS * EXPERTS_PER_GROUP
LN_EPS = 1e-5
NEG = -1e30

LANES = 128
ROUTE_LANE0 = N_EXPERT_GROUPS
VMEM_LIMIT = 56 * 1024 * 1024

EXPERT_TILE = 512
ROW_CHUNK = 16
ROW_UNROLL = 4


def _alibi_slopes(n):
    return [float(v) for v in 2.0 ** (-8.0 * np.arange(1, n + 1, dtype=np.float32) / n)]


def _params(*sem):
    return pltpu.CompilerParams(dimension_semantics=sem, vmem_limit_bytes=VMEM_LIMIT)


def _ada_kernel(c_ref, w_ref, b_ref, o_ref):
    c = c_ref[...]
    cond = c * jax.nn.sigmoid(c)
    o_ref[0] = jnp.dot(cond, w_ref[0], preferred_element_type=F32, precision=HIGHEST) + b_ref[0]


def _ada(c, w_ada, b_ada):
    depth, d, n = w_ada.shape
    bsz = c.shape[0]
    tn = 1024
    return pl.pallas_call(
        _ada_kernel,
        grid=(depth, n // tn),
        in_specs=[pl.BlockSpec((bsz, d), lambda l, j: (0, 0)),
                  pl.BlockSpec((1, d, tn), lambda l, j: (l, 0, j)),
                  pl.BlockSpec((1, 1, tn), lambda l, j: (l, 0, j))],
        out_specs=pl.BlockSpec((1, bsz, tn), lambda l, j: (l, 0, j)),
        out_shape=jax.ShapeDtypeStruct((depth, bsz, n), F32),
        compiler_params=_params("arbitrary", "arbitrary"),
        name="ada",
    )(c, w_ada, b_ada.reshape(depth, 1, n))


def _modulate_kernel(x_ref, sh_ref, sc_ref, o_ref):
    o_ref[...] = (x_ref[...] * (1.0 + sc_ref[0]) + sh_ref[0]).astype(o_ref.dtype)


def _modulate(x2, ada_l, seq, tm=512):
    t, d = x2.shape
    per_b = seq // tm
    return pl.pallas_call(
        _modulate_kernel,
        grid=(t // tm,),
        in_specs=[pl.BlockSpec((tm, d), lambda i: (i, 0)),
                  pl.BlockSpec((1, 1, d), lambda i: (i // per_b, 0, 0)),
                  pl.BlockSpec((1, 1, d), lambda i: (i // per_b, 0, 1))],
        out_specs=pl.BlockSpec((tm, d), lambda i: (i, 0)),
        out_shape=jax.ShapeDtypeStruct((t, d), BF16),
        compiler_params=_params("arbitrary"),
        name="modulate",
    )(x2, ada_l, ada_l)


def _mm_kernel(x_ref, w_ref, o_ref):
    o_ref[...] = jnp.dot(x_ref[...], w_ref[0], preferred_element_type=F32).astype(o_ref.dtype)


def _matmul(x2, w, layer, col0, n, tm=1024, tn=768):
    t, d = x2.shape
    tm = min(tm, t)
    c0 = col0 // tn
    assert col0 % tn == 0 and n % tn == 0
    return pl.pallas_call(
        _mm_kernel,
        grid=(t // tm, n // tn),
        in_specs=[pl.BlockSpec((tm, d), lambda i, j: (i, 0)),
                  pl.BlockSpec((1, d, tn), lambda i, j: (layer, 0, c0 + j))],
        out_specs=pl.BlockSpec((tm, tn), lambda i, j: (i, j)),
        out_shape=jax.ShapeDtypeStruct((t, n), BF16),
        compiler_params=_params("arbitrary", "arbitrary"),
        name="proj",
    )(x2, w)


def _gate_kernel(x_ref, w_ref, b_ref, o_ref):
    acc = jnp.dot(x_ref[...], w_ref[0, 0], preferred_element_type=F32) + b_ref[0, 0]
    o_ref[...] = jax.nn.sigmoid(acc).astype(o_ref.dtype)


def _gates(x2, w_gate, b_gate, layer, tm=1024, tn=1024):
    t, d = x2.shape
    _, nbr, _, n = w_gate.shape
    tm = min(tm, t)
    per = n // tn
    return pl.pallas_call(
        _gate_kernel,
        grid=(t // tm, nbr * per),
        in_specs=[pl.BlockSpec((tm, d), lambda i, j: (i, 0)),
                  pl.BlockSpec((1, 1, d, tn), lambda i, j: (layer, j // per, 0, j % per)),
                  pl.BlockSpec((1, 1, 1, tn), lambda i, j: (layer, j // per, 0, j % per))],
        out_specs=pl.BlockSpec((tm, tn), lambda i, j: (i, j)),
        out_shape=jax.ShapeDtypeStruct((t, nbr * n), BF16),
        compiler_params=_params("arbitrary", "arbitrary"),
        name="gates",
    )(x2, w_gate, b_gate)


def _attn_a_kernel(q_ref, kp_ref, kc_ref, vp_ref, vc_ref, o_ref, lse_ref, kbuf, vbuf, *, slopes, dist_unit, nb):
    n = pl.program_id(1)
    kbuf[0:BLOCK, :] = kp_ref[0]
    kbuf[BLOCK:, :] = kc_ref[0]
    vbuf[0:BLOCK, :] = vp_ref[0]
    vbuf[BLOCK:, :] = vc_ref[0]
    row = lax.broadcasted_iota(jnp.int32, (BLOCK, 2 * BLOCK), 0)
    col = lax.broadcasted_iota(jnp.int32, (BLOCK, 2 * BLOCK), 1)
    dist = row + BLOCK - col
    band = (dist >= 0) & (dist <= BLOCK)
    first_lim = jnp.where(n > 0, 0, BLOCK)
    band_first = band & (col >= first_lim)
    distf = dist.astype(F32) * float(dist_unit)
    scale = HEAD_DIM_A ** -0.5
    for h in range(HEADS_A):
        hs = slice(h * HEAD_DIM_A, (h + 1) * HEAD_DIM_A)
        bias = -slopes[h] * distf
        for i in range(nb // BLOCK):
            rows = slice(i * BLOCK, (i + 1) * BLOCK)
            q = q_ref[0, rows, hs]
            k = kbuf[i * BLOCK:(i + 2) * BLOCK, hs]
            v = vbuf[i * BLOCK:(i + 2) * BLOCK, hs]
            s = lax.dot_general(q, k, (((1,), (1,)), ((), ())), preferred_element_type=F32) * scale + bias
            s = jnp.where(band_first if i == 0 else band, s, NEG)
            m = jnp.max(s, axis=-1, keepdims=True)
            p = jnp.exp(s - m)
            l = jnp.sum(p, axis=-1, keepdims=True)
            o = jnp.dot(p.astype(BF16), v, preferred_element_type=F32) / l
            o_ref[0, rows, hs] = o.astype(o_ref.dtype)
            lse_ref[0, rows, h:h + 1] = m + jnp.log(l)


def _attn_a_group(qkv, cols, slopes, dist_unit):
    nseq, l, _ = qkv.shape
    nb = min(512, l)
    sub = nb // BLOCK
    cq, ck, cv = cols
    w = GROUP_WIDTH_A
    kern = functools.partial(_attn_a_kernel, slopes=slopes, dist_unit=dist_unit, nb=nb)
    prev = lambda b, n: jnp.maximum(n * sub - 1, 0)
    return pl.pallas_call(
        kern,
        grid=(nseq, l // nb),
        in_specs=[pl.BlockSpec((1, nb, w), lambda b, n: (b, n, cq)),
                  pl.BlockSpec((1, BLOCK, w), lambda b, n: (b, prev(b, n), ck)),
                  pl.BlockSpec((1, nb, w), lambda b, n: (b, n, ck)),
                  pl.BlockSpec((1, BLOCK, w), lambda b, n: (b, prev(b, n), cv)),
                  pl.BlockSpec((1, nb, w), lambda b, n: (b, n, cv))],
        out_specs=[pl.BlockSpec((1, nb, w), lambda b, n: (b, n, 0)),
                   pl.BlockSpec((1, nb, HEADS_A), lambda b, n: (b, n, 0))],
        out_shape=[jax.ShapeDtypeStruct((nseq, l, w), BF16),
                   jax.ShapeDtypeStruct((nseq, l, HEADS_A), F32)],
        scratch_shapes=[pltpu.VMEM((BLOCK + nb, w), BF16), pltpu.VMEM((BLOCK + nb, w), BF16)],
        compiler_params=_params("arbitrary", "arbitrary"),
        name="attn_a",
    )(qkv, qkv, qkv, qkv, qkv)


def _mixer_a(proj_a):
    b, s, _ = proj_a.shape
    slopes = _alibi_slopes(N_GROUPS_A * HEADS_A)
    outs, lses = [], []
    for g, (window, r) in enumerate(DILATED_GROUPS):
        sl = slopes[g * HEADS_A:(g + 1) * HEADS_A]
        assert window // r == BLOCK
        if r == 1:
            o, lse = _attn_a_group(proj_a, (g, 3 + g, 6 + g), sl, 1)
        else:
            parts = [proj_a[:, :, p * QKV_A + g * GROUP_WIDTH_A: p * QKV_A + (g + 1) * GROUP_WIDTH_A] for p in range(3)]
            qkv = jnp.concatenate(parts, axis=-1)
            qkv = qkv.reshape(b, s // r, r, 3 * GROUP_WIDTH_A).transpose(0, 2, 1, 3).reshape(b * r, s // r, 3 * GROUP_WIDTH_A)
            o, lse = _attn_a_group(qkv, (0, 1, 2), sl, r)
            o = o.reshape(b, r, s // r, GROUP_WIDTH_A).transpose(0, 2, 1, 3).reshape(b, s, GROUP_WIDTH_A)
            lse = lse.reshape(b, r, s // r, HEADS_A).transpose(0, 2, 1, 3).reshape(b, s, HEADS_A)
        outs.append(o)
        lses.append(lse)
    return outs, lses


def _attn_b_kernel(q_ref, k_ref, v_ref, lv_ref, g_ref, o_ref, kt_buf, s_buf, m_buf, l_buf, acc_buf, *,
                   slopes, lambda_init, tq):
    h = pl.program_id(1)
    qi = pl.program_id(2)
    d = HEAD_DIM_B
    nk = kt_buf.shape[1]
    nt = (((1,), (1,)), ((), ()))

    @pl.when(qi == 0)
    def _():
        eye = (lax.broadcasted_iota(jnp.int32, (d, d), 0) == lax.broadcasted_iota(jnp.int32, (d, d), 1)).astype(BF16)

        def transpose(c, carry):
            k = k_ref[0, pl.ds(pl.multiple_of(c * tq, tq), tq), :]
            for j in range(2):
                kt = lax.dot_general(eye, k[:, j * d:(j + 1) * d], nt, preferred_element_type=F32)
                kt_buf[j, c] = kt.astype(BF16)
            return carry

        lax.fori_loop(0, nk, transpose, 0)

    slope = jnp.float32(slopes[0])
    for j in range(1, HEADS_B):
        slope = jnp.where(h == j, jnp.float32(slopes[j]), slope)
    q = q_ref[0]
    qs = [(q[:, j * d:(j + 1) * d].astype(F32) * (d ** -0.5)).astype(BF16) for j in range(2)]
    colf = lax.broadcasted_iota(jnp.int32, (1, tq), 1).astype(F32)
    nl = tq // LANES

    def lane_fold(x, op, acc):
        for c in range(nl):
            acc = op(acc, x[:, c * LANES:(c + 1) * LANES])
        return acc

    m_buf[...] = jnp.full(m_buf.shape, NEG, F32)

    def scores(kb, masked):
        bias = slope * (colf + ((kb - qi) * tq).astype(F32))
        for j in range(2):
            s = jnp.dot(qs[j], kt_buf[j, kb], preferred_element_type=F32) + bias
            if masked:
                row = lax.broadcasted_iota(jnp.int32, (tq, tq), 0)
                col = lax.broadcasted_iota(jnp.int32, (tq, tq), 1)
                s = jnp.where(col <= row, s, NEG)
            s_buf[j, kb] = s
            m_buf[j] = lane_fold(s, jnp.maximum, m_buf[j])

    def scores_body(kb, carry):
        scores(kb, False)
        return carry

    lax.fori_loop(0, qi, scores_body, 0)
    scores(qi, True)
    ms = [jnp.max(m_buf[j], axis=-1, keepdims=True) for j in range(2)]

    l_buf[...] = jnp.zeros_like(l_buf)
    acc_buf[...] = jnp.zeros_like(acc_buf)

    def weights(kb, carry):
        v = v_ref[0, pl.ds(pl.multiple_of(kb * tq, tq), tq), :]
        for j in range(2):
            p = jnp.exp(s_buf[j, kb] - ms[j])
            l_buf[j] = lane_fold(p, jnp.add, l_buf[j])
            acc_buf[j] = acc_buf[j] + jnp.dot(p.astype(BF16), v, preferred_element_type=F32)
        return carry

    lax.fori_loop(0, qi + 1, weights, 0)
    l0, l1 = (jnp.sum(l_buf[j], axis=-1, keepdims=True) for j in range(2))
    a0, a1 = acc_buf[0], acc_buf[1]
    lv = lv_ref[...]
    lam = (jnp.exp(jnp.sum(lv[0:1] * lv[1:2], keepdims=True)) - jnp.exp(jnp.sum(lv[2:3] * lv[3:4], keepdims=True))
           + lambda_init)
    o = a0 / l0 - lam * (a1 / l1)
    o = o * lax.rsqrt(jnp.mean(o * o, axis=-1, keepdims=True) + LN_EPS)
    o = o * g_ref[...] * (1.0 - lambda_init)
    o_ref[0] = o.astype(o_ref.dtype)


def _mixer_b(proj_bc, lam_vecs, subln_g, lambda_init, tq=512):
    b, s, _ = proj_bc.shape
    w = 2 * HEAD_DIM_B
    nk = s // tq
    kern = functools.partial(_attn_b_kernel, slopes=_alibi_slopes(HEADS_B), lambda_init=lambda_init, tq=tq)
    return pl.pallas_call(
        kern,
        grid=(b, HEADS_B, nk),
        in_specs=[pl.BlockSpec((1, tq, w), lambda bi, h, i: (bi, i, h)),
                  pl.BlockSpec((1, s, w), lambda bi, h, i: (bi, 0, HEADS_B + h)),
                  pl.BlockSpec((1, s, w), lambda bi, h, i: (bi, 0, 2 * HEADS_B + h)),
                  pl.BlockSpec(lam_vecs.shape, lambda bi, h, i: (0, 0)),
                  pl.BlockSpec((1, w), lambda bi, h, i: (0, 0))],
        out_specs=pl.BlockSpec((1, tq, w), lambda bi, h, i: (bi, i, h)),
        out_shape=jax.ShapeDtypeStruct((b, s, HEADS_B * w), BF16),
        scratch_shapes=[pltpu.VMEM((2, nk, HEAD_DIM_B, tq), BF16), pltpu.VMEM((2, nk, tq, tq), F32),
                        pltpu.VMEM((2, tq, LANES), F32), pltpu.VMEM((2, tq, LANES), F32),
                        pltpu.VMEM((2, tq, w), F32)],
        compiler_params=_params("arbitrary", "arbitrary", "arbitrary"),
        name="attn_b",
    )(proj_bc, proj_bc, proj_bc, lam_vecs, subln_g.reshape(1, w))


def _attn_c_kernel(sink_ref, q_ref, kp_ref, kc_ref, vp_ref, vc_ref, o_ref, kbuf, vbuf, *, slopes, nb):
    n = pl.program_id(1)
    kbuf[0:BLOCK, :] = kp_ref[0]
    kbuf[BLOCK:, :] = kc_ref[0]
    vbuf[0:BLOCK, :] = vp_ref[0]
    vbuf[BLOCK:, :] = vc_ref[0]
    row = lax.broadcasted_iota(jnp.int32, (BLOCK, 2 * BLOCK), 0)
    col = lax.broadcasted_iota(jnp.int32, (BLOCK, 2 * BLOCK), 1)
    dist = row + BLOCK - col
    band = (dist >= 0) & (dist <= WINDOW_C - 1)
    first_lim = jnp.where(n > 0, 0, BLOCK)
    band_first = band & (col >= first_lim)
    distf = dist.astype(F32)
    d = HEAD_DIM_C
    scale = d ** -0.5
    for h in range(HEADS_C):
        kv = h // GQA_GROUP_C
        bias = -slopes[h] * distf
        sink = sink_ref[h]
        for i in range(nb // BLOCK):
            rows = slice(i * BLOCK, (i + 1) * BLOCK)
            q = q_ref[0, rows, h * d:(h + 1) * d]
            k = kbuf[i * BLOCK:(i + 2) * BLOCK, kv * d:(kv + 1) * d]
            v = vbuf[i * BLOCK:(i + 2) * BLOCK, kv * d:(kv + 1) * d]
            s = lax.dot_general(q, k, (((1,), (1,)), ((), ())), preferred_element_type=F32) * scale + bias
            s = jnp.where(band_first if i == 0 else band, s, NEG)
            m = jnp.maximum(jnp.max(s, axis=-1, keepdims=True), sink)
            p = jnp.exp(s - m)
            denom = jnp.sum(p, axis=-1, keepdims=True) + jnp.exp(sink - m)
            o = jnp.dot(p.astype(BF16), v, preferred_element_type=F32) / denom
            o_ref[0, rows, h * d:(h + 1) * d] = o.astype(o_ref.dtype)


def _mixer_c(proj_bc, sinks):
    b, s, _ = proj_bc.shape
    nb = min(512, s)
    sub = nb // BLOCK
    wq = HEADS_C * HEAD_DIM_C
    wkv = KV_HEADS_C * HEAD_DIM_C
    cq, ck, cv = 1536 // wq, 2048 // wkv, 2176 // wkv
    kern = functools.partial(_attn_c_kernel, slopes=_alibi_slopes(HEADS_C), nb=nb)
    prev = lambda n: jnp.maximum(n * sub - 1, 0)
    return pl.pallas_call(
        kern,
        grid=(b, s // nb),
        in_specs=[pl.BlockSpec(memory_space=pltpu.SMEM),
                  pl.BlockSpec((1, nb, wq), lambda bi, n: (bi, n, cq)),
                  pl.BlockSpec((1, BLOCK, wkv), lambda bi, n: (bi, prev(n), ck)),
                  pl.BlockSpec((1, nb, wkv), lambda bi, n: (bi, n, ck)),
                  pl.BlockSpec((1, BLOCK, wkv), lambda bi, n: (bi, prev(n), cv)),
                  pl.BlockSpec((1, nb, wkv), lambda bi, n: (bi, n, cv))],
        out_specs=pl.BlockSpec((1, nb, wq), lambda bi, n: (bi, n, 0)),
        out_shape=jax.ShapeDtypeStruct((b, s, wq), BF16),
        scratch_shapes=[pltpu.VMEM((BLOCK + nb, wkv), BF16), pltpu.VMEM((BLOCK + nb, wkv), BF16)],
        compiler_params=_params("arbitrary", "arbitrary"),
        name="attn_c",
    )(sinks, proj_bc, proj_bc, proj_bc, proj_bc, proj_bc)


def _merge_kernel(o1_ref, o2_ref, o3_ref, l1_ref, l2_ref, l3_ref, yb_ref, yc_ref, g_ref, wb_ref, out_ref):
    la, lb, lc = l1_ref[...], l2_ref[...], l3_ref[...]
    m = jnp.maximum(jnp.maximum(la, lb), lc)
    ea, eb, ec = jnp.exp(la - m), jnp.exp(lb - m), jnp.exp(lc - m)
    den = ea + eb + ec
    wa, wb, wc = ea / den, eb / den, ec / den
    parts = []
    for h in range(HEADS_A):
        hs = slice(h * HEAD_DIM_A, (h + 1) * HEAD_DIM_A)
        y = (wa[:, h:h + 1] * o1_ref[:, hs].astype(F32) + wb[:, h:h + 1] * o2_ref[:, hs].astype(F32)
             + wc[:, h:h + 1] * o3_ref[:, hs].astype(F32))
        parts.append(y.astype(BF16))
    ya = jnp.concatenate(parts, axis=1)
    dm = out_ref.shape[1]
    acc = g_ref[:, 0:dm].astype(F32) * jnp.dot(ya, wb_ref[0, 0], preferred_element_type=F32)
    acc = acc + g_ref[:, dm:2 * dm].astype(F32) * jnp.dot(yb_ref[...], wb_ref[0, 1], preferred_element_type=F32)
    acc = acc + g_ref[:, 2 * dm:3 * dm].astype(F32) * jnp.dot(yc_ref[...], wb_ref[0, 2], preferred_element_type=F32)
    out_ref[...] = acc.astype(out_ref.dtype)


def _merge(outs_a, lses_a, y_b, y_c, gates, w_branch, layer, tm=512):
    t = y_b.shape[0]
    dm = w_branch.shape[3]
    wide = lambda w: pl.BlockSpec((tm, w), lambda i: (i, 0))
    return pl.pallas_call(
        _merge_kernel,
        grid=(t // tm,),
        in_specs=[wide(BRANCH_WIDTH)] * 3 + [wide(HEADS_A)] * 3 + [wide(BRANCH_WIDTH)] * 2 + [wide(3 * dm)]
                 + [pl.BlockSpec((1,) + w_branch.shape[1:], lambda i: (layer, 0, 0, 0))],
        out_specs=wide(dm),
        out_shape=jax.ShapeDtypeStruct((t, dm), BF16),
        compiler_params=_params("arbitrary"),
        name="merge",
    )(*outs_a, *lses_a, y_b, y_c, gates, w_branch)


def _layer_norm(z, g, b):
    mu = jnp.mean(z, axis=-1, keepdims=True)
    zc = z - mu
    var = jnp.mean(zc * zc, axis=-1, keepdims=True)
    return zc * lax.rsqrt(var + LN_EPS) * g + b


def _wo_ln_route_kernel(a_ref, wo_ref, x_ref, g1_ref, sh_ref, sc_ref, lng_ref, lnb_ref, whi_ref, wlo_ref, br_ref,
                        xo_ref, route_ref, cnt_ref, run_ref, mix_buf, hhi_buf, hlo_buf, *, alpha, tm):
    @pl.when(pl.program_id(0) == 0)
    def _():
        run_ref[...] = jnp.zeros_like(run_ref)

    mix_buf[...] = jnp.dot(a_ref[...], wo_ref[0], preferred_element_type=F32)

    def rows(r, carry):
        rs = pl.ds(pl.multiple_of(r * ROW_CHUNK, ROW_CHUNK), ROW_CHUNK)
        xn = _layer_norm(alpha * x_ref[rs, :] + g1_ref[0] * mix_buf[rs, :], lng_ref[...], lnb_ref[...])
        xo_ref[rs, :] = xn
        h = xn * (1.0 + sc_ref[0]) + sh_ref[0]
        hi = h.astype(BF16)
        hhi_buf[rs, :] = hi
        hlo_buf[rs, :] = (h - hi.astype(F32)).astype(BF16)
        return carry

    lax.fori_loop(0, tm // ROW_CHUNK, rows, 0, unroll=ROW_UNROLL)

    hhi = hhi_buf[...]
    logits = (jnp.dot(hhi, whi_ref[...], preferred_element_type=F32)
              + jnp.dot(hlo_buf[...], whi_ref[...], preferred_element_type=F32)
              + jnp.dot(hhi, wlo_ref[...], preferred_element_type=F32) + br_ref[...])
    lane = lax.broadcasted_iota(jnp.int32, (tm, LANES), 1).astype(F32)
    far = float(4 * LANES)
    gl = jnp.where(lane < N_EXPERT_GROUPS, logits, NEG)
    gm = jnp.max(gl, axis=-1, keepdims=True)
    g_val = 1.0 / jnp.sum(jnp.exp(gl - gm), axis=-1, keepdims=True)
    g_idx = jnp.min(jnp.where(gl == gm, lane, far), axis=-1, keepdims=True)
    lo = ROUTE_LANE0 + EXPERTS_PER_GROUP * g_idx
    fl = jnp.where((lane >= lo) & (lane < lo + EXPERTS_PER_GROUP), logits, NEG)
    m1 = jnp.max(fl, axis=-1, keepdims=True)
    i1 = jnp.min(jnp.where(fl == m1, lane, far), axis=-1, keepdims=True)
    fl2 = jnp.where(lane == i1, NEG, fl)
    m2 = jnp.max(fl2, axis=-1, keepdims=True)
    i2 = jnp.min(jnp.where(fl2 == m2, lane, far), axis=-1, keepdims=True)
    e = jnp.exp(m2 - m1)
    w1 = g_val / (1.0 + e)
    w2 = g_val * e / (1.0 + e)

    oh1 = lane == i1
    oh2 = lane == i2
    c = jnp.where(oh1 | oh2, 1.0, 0.0)
    r_i = lax.broadcasted_iota(jnp.int32, (tm, tm), 0)
    c_i = lax.broadcasted_iota(jnp.int32, (tm, tm), 1)
    before = jnp.where(c_i < r_i, 1.0, 0.0).astype(BF16)
    tot = jnp.dot(before, c.astype(BF16), preferred_element_type=F32) + run_ref[...]
    rank1 = jnp.sum(jnp.where(oh1, tot, 0.0), axis=-1, keepdims=True)
    rank2 = jnp.sum(jnp.where(oh2, tot, 0.0), axis=-1, keepdims=True)
    run_ref[...] = run_ref[...] + jnp.sum(c, axis=0, keepdims=True)
    cnt_ref[...] = run_ref[...]

    l8 = lax.broadcasted_iota(jnp.int32, (tm, 8), 1)
    e1 = i1 - ROUTE_LANE0
    e2 = i2 - ROUTE_LANE0
    route = jnp.where(l8 == 0, w1, jnp.where(l8 == 1, w2, jnp.where(l8 == 2, e1, jnp.where(l8 == 3, e2,
            jnp.where(l8 == 4, rank1, rank2)))))
    route_ref[...] = route


def _wo_ln_route(merged, w_o, layer, x2, ada_l, ln_g, ln_b, w_r_hi, w_r_lo, b_r, seq, alpha, tm=256):
    t, d = x2.shape
    per_b = seq // tm
    vec = lambda k: pl.BlockSpec((1, 1, d), lambda i: (i // per_b, 0, k))
    row = lambda w: pl.BlockSpec((tm, w), lambda i: (i, 0))
    full = lambda a: pl.BlockSpec(a.shape, lambda i: (0,) * a.ndim)
    kern = functools.partial(_wo_ln_route_kernel, alpha=alpha, tm=tm)
    return pl.pallas_call(
        kern,
        grid=(t // tm,),
        in_specs=[row(d), pl.BlockSpec((1, d, d), lambda i: (layer, 0, 0)), row(d), vec(2), vec(3), vec(4),
                  full(ln_g), full(ln_b), full(w_r_hi), full(w_r_lo), full(b_r)],
        out_specs=[row(d), row(8), pl.BlockSpec((1, LANES), lambda i: (0, 0))],
        out_shape=[jax.ShapeDtypeStruct((t, d), F32), jax.ShapeDtypeStruct((t, 8), F32),
                   jax.ShapeDtypeStruct((1, LANES), F32)],
        scratch_shapes=[pltpu.VMEM((1, LANES), F32), pltpu.VMEM((tm, d), F32), pltpu.VMEM((tm, d), BF16),
                        pltpu.VMEM((tm, d), BF16)],
        compiler_params=_params("arbitrary"),
        name="wo_ln_route",
    )(merged, w_o, x2, ada_l, ada_l, ada_l, ln_g, ln_b, w_r_hi, w_r_lo, b_r)


def _row_copy(src, i, dst, j, sem):
    return pltpu.make_async_copy(src.at[pl.ds(i, 1), :], dst.at[pl.ds(j, 1), :], sem)


def _dispatch_kernel(pos_ref, x_ref, sh_ref, sc_ref, xs_hbm, hbuf, sem, *, tc):
    step = pl.program_id(0)
    slot = step % 2
    stage = hbuf.at[slot]

    def rows(r, carry):
        rs = pl.ds(pl.multiple_of(r * ROW_CHUNK, ROW_CHUNK), ROW_CHUNK)
        stage[rs, :] = x_ref[rs, :] * (1.0 + sc_ref[0]) + sh_ref[0]
        return carry

    lax.fori_loop(0, tc // ROW_CHUNK, rows, 0, unroll=ROW_UNROLL)

    def issue(i, c):
        _row_copy(stage, i, xs_hbm, pos_ref[2 * i], sem.at[slot]).start()
        _row_copy(stage, i, xs_hbm, pos_ref[2 * i + 1], sem.at[slot]).start()
        return c

    lax.fori_loop(0, tc, issue, 0, unroll=4)

    def drain(which):
        def body(i, c):
            _row_copy(hbuf.at[which], 0, xs_hbm, 0, sem.at[which]).wait()
            return c

        lax.fori_loop(0, 2 * tc, body, 0, unroll=8)

    @pl.when(step > 0)
    def _():
        drain(1 - slot)

    @pl.when(step == pl.num_programs(0) - 1)
    def _():
        drain(slot)


def _dispatch(x2, ada_l, pos, seq, tc=256):
    t, d = x2.shape
    per_b = seq // tc
    vec = lambda k: pl.BlockSpec((1, 1, d), lambda i: (i // per_b, 0, k))
    return pl.pallas_call(
        functools.partial(_dispatch_kernel, tc=tc),
        grid=(t // tc,),
        in_specs=[pl.BlockSpec((2 * tc,), lambda i: (i,), memory_space=pltpu.SMEM),
                  pl.BlockSpec((tc, d), lambda i: (i, 0)), vec(3), vec(4)],
        out_specs=pl.BlockSpec(memory_space=pl.ANY),
        out_shape=jax.ShapeDtypeStruct((2 * t, d), F32),
        scratch_shapes=[pltpu.VMEM((2, tc, d), F32), pltpu.SemaphoreType.DMA((2,))],
        compiler_params=_params("arbitrary"),
        name="dispatch",
    )(pos, x2, ada_l, ada_l)


def _expert_kernel(it_ref, ie_ref, lo_ref, hi_ref, first_ref, nu_ref, x_ref, w1_ref, w3_ref, w2_ref, y_ref, *, tile):
    i = pl.program_id(0)

    @pl.when(i < nu_ref[0])
    def _():
        x = x_ref[...].astype(BF16)
        a = jnp.dot(x, w1_ref[0, 0], preferred_element_type=F32)
        b = jnp.dot(x, w3_ref[0, 0], preferred_element_type=F32)
        he = (a * jax.nn.sigmoid(a) * b).astype(BF16)
        y = jnp.dot(he, w2_ref[0, 0], preferred_element_type=F32)
        row = it_ref[i] * tile + lax.broadcasted_iota(jnp.int32, (tile, 1), 0)
        mine = (row >= lo_ref[i]) & (row < hi_ref[i])

        @pl.when(first_ref[i] == 1)
        def _():
            y_ref[...] = jnp.where(mine, y, 0.0)

        @pl.when(first_ref[i] == 0)
        def _():
            y_ref[...] = jnp.where(mine, y, y_ref[...])


def _experts(xs, w1, w3, w2, layer, plan):
    n_rows, d = xs.shape
    de = w1.shape[3]
    tile = EXPERT_TILE
    n_items = plan[0].shape[0]
    row_map = lambda i, it, ie, lo, hi, fi, nu: (it[i], 0)
    w_map = lambda i, it, ie, lo, hi, fi, nu: (layer, ie[i], 0, 0)
    grid_spec = pltpu.PrefetchScalarGridSpec(
        num_scalar_prefetch=6,
        grid=(n_items,),
        in_specs=[pl.BlockSpec((tile, d), row_map),
                  pl.BlockSpec((1, 1, d, de), w_map),
                  pl.BlockSpec((1, 1, d, de), w_map),
                  pl.BlockSpec((1, 1, de, d), w_map)],
        out_specs=pl.BlockSpec((tile, d), row_map),
    )
    return pl.pallas_call(
        functools.partial(_expert_kernel, tile=tile),
        grid_spec=grid_spec,
        out_shape=jax.ShapeDtypeStruct((n_rows, d), F32),
        compiler_params=_params("arbitrary"),
        name="experts",
    )(*plan, xs, w1, w3, w2)


def _combine_kernel(pos_ref, posn_ref, ys_hbm, route_ref, x_ref, g2_ref, lng_ref, lnb_ref, sh_ref, sc_ref,
                    xo_ref, h_ref, buf, sem, *, alpha, tc):
    step = pl.program_id(0)
    slot = step % 2

    def gather(p_ref, s):
        def issue(i, c):
            _row_copy(ys_hbm, p_ref[2 * i], buf.at[s, 0], i, sem.at[s]).start()
            _row_copy(ys_hbm, p_ref[2 * i + 1], buf.at[s, 1], i, sem.at[s]).start()
            return c

        lax.fori_loop(0, tc, issue, 0, unroll=4)

    @pl.when(step == 0)
    def _():
        gather(pos_ref, 0)

    @pl.when(step + 1 < pl.num_programs(0))
    def _():
        gather(posn_ref, 1 - slot)

    def drain(i, c):
        _row_copy(ys_hbm, 0, buf.at[slot, 0], 0, sem.at[slot]).wait()
        return c

    lax.fori_loop(0, 2 * tc, drain, 0, unroll=8)

    def rows(r, carry):
        rs = pl.ds(pl.multiple_of(r * ROW_CHUNK, ROW_CHUNK), ROW_CHUNK)
        w = route_ref[rs, :]
        ffn = w[:, 0:1] * buf[slot, 0, rs, :] + w[:, 1:2] * buf[slot, 1, rs, :]
        xn = _layer_norm(alpha * x_ref[rs, :] + g2_ref[0] * ffn, lng_ref[...], lnb_ref[...])
        xo_ref[rs, :] = xn
        h_ref[rs, :] = (xn * (1.0 + sc_ref[0]) + sh_ref[0]).astype(h_ref.dtype)
        return carry

    lax.fori_loop(0, tc // ROW_CHUNK, rows, 0, unroll=ROW_UNROLL)


def _combine(ys, pos, route, x2, ada_l, ada_next, ln_g, ln_b, seq, alpha, tc=256):
    t, d = x2.shape
    per_b = seq // tc
    n = t // tc
    vec = lambda k: pl.BlockSpec((1, 1, d), lambda i: (i // per_b, 0, k))
    row = lambda w: pl.BlockSpec((tc, w), lambda i: (i, 0))
    full = lambda a: pl.BlockSpec(a.shape, lambda i: (0,) * a.ndim)
    kern = functools.partial(_combine_kernel, alpha=alpha, tc=tc)
    return pl.pallas_call(
        kern,
        grid=(n,),
        in_specs=[pl.BlockSpec((2 * tc,), lambda i: (i,), memory_space=pltpu.SMEM),
                  pl.BlockSpec((2 * tc,), lambda i: (jnp.minimum(i + 1, n - 1),), memory_space=pltpu.SMEM),
                  pl.BlockSpec(memory_space=pl.ANY),
                  row(8), row(d), vec(5), full(ln_g), full(ln_b), vec(0), vec(1)],
        out_specs=[row(d), row(d)],
        out_shape=[jax.ShapeDtypeStruct((t, d), F32), jax.ShapeDtypeStruct((t, d), BF16)],
        scratch_shapes=[pltpu.VMEM((2, 2, tc, d), F32), pltpu.SemaphoreType.DMA((2,))],
        compiler_params=_params("arbitrary"),
        name="combine",
    )(pos, pos, ys, route, x2, ada_l, ln_g, ln_b, ada_next, ada_next)


def _positions_kernel(route_ref, starts_ref, pos_ref):
    r = route_ref[...]
    tm = r.shape[0]
    lane = lax.broadcasted_iota(jnp.int32, (tm, LANES), 1).astype(F32)
    starts = starts_ref[...]

    def start_of(e):
        return jnp.sum(jnp.where(lane == e + ROUTE_LANE0, starts, 0.0), axis=-1, keepdims=True)

    p1 = start_of(r[:, 2:3]) + r[:, 4:5]
    p2 = start_of(r[:, 3:4]) + r[:, 5:6]
    l2 = lax.broadcasted_iota(jnp.int32, (tm, 2), 1)
    pos_ref[...] = jnp.where(l2 == 0, p1, p2).astype(jnp.int32)


def _positions(route, starts, tm=2048):
    t = route.shape[0]
    tm = min(tm, t)
    return pl.pallas_call(
        _positions_kernel,
        grid=(t // tm,),
        in_specs=[pl.BlockSpec((tm, 8), lambda i: (i, 0)), pl.BlockSpec((1, LANES), lambda i: (0, 0))],
        out_specs=pl.BlockSpec((tm, 2), lambda i: (i, 0)),
        out_shape=jax.ShapeDtypeStruct((t, 2), jnp.int32),
        compiler_params=_params("arbitrary"),
        name="positions",
    )(route, starts)


def _dispatch_plan(counts, n_rows):
    tile = EXPERT_TILE
    n_items = n_rows // tile + N_EXPERTS
    starts_lanes = jnp.cumsum(counts, axis=1) - counts
    cnt = counts[0, ROUTE_LANE0:ROUTE_LANE0 + N_EXPERTS].astype(jnp.int32)
    ends = jnp.cumsum(cnt)
    starts = ends - cnt
    first_tile = starts // tile
    n_e = jnp.where(cnt > 0, (ends - 1) // tile - first_tile + 1, 0)
    item_ends = jnp.cumsum(n_e)
    n_used = item_ends[-1]
    k = jnp.minimum(jnp.arange(n_items, dtype=jnp.int32), n_used - 1)
    item_expert = jnp.minimum(jnp.sum((item_ends[None, :] <= k[:, None]).astype(jnp.int32), axis=1), N_EXPERTS - 1)
    onehot = (item_expert[:, None] == jnp.arange(N_EXPERTS, dtype=jnp.int32)[None, :]).astype(jnp.int32)
    pick = lambda v: jnp.sum(onehot * v[None, :], axis=1)
    item_tile = pick(first_tile) + k - pick(item_ends - n_e)
    item_first = jnp.concatenate([jnp.ones((1,), jnp.int32), (item_tile[1:] != item_tile[:-1]).astype(jnp.int32)])
    plan = (item_tile, item_expert, pick(starts), pick(ends), item_first, n_used.reshape(1))
    return starts_lanes, tuple(p.astype(jnp.int32) for p in plan)


def kernel(x, c, w_ada, b_ada, w_in, w_gate, b_gate, w_branch, w_o, lam_vecs, subln_g, sinks, ln_g, ln_b,
           w_rg, b_rg, w_rf, b_rf, w1, w3, w2):
    b, s, d = x.shape
    depth = w_ada.shape[0]
    t = b * s
    alpha = (2 * depth) ** 0.25
    assert s % (16 * BLOCK) == 0 and d % LANES == 0

    w_in16 = w_in.astype(BF16)
    w_gate16 = w_gate.astype(BF16)
    w_branch16 = w_branch.astype(BF16)
    w_o16 = w_o.astype(BF16)
    w1_16, w3_16, w2_16 = w1.astype(BF16), w3.astype(BF16), w2.astype(BF16)
    pad = LANES - N_EXPERT_GROUPS - N_EXPERTS
    w_r = jnp.concatenate([w_rg, w_rf, jnp.zeros((depth, d, pad), F32)], axis=-1)
    w_r_hi = w_r.astype(BF16)
    w_r_lo = (w_r - w_r_hi.astype(F32)).astype(BF16)
    b_r = jnp.concatenate([b_rg, b_rf, jnp.zeros((depth, pad), F32)], axis=-1).reshape(depth, 1, LANES)

    b_gate4 = b_gate.reshape(depth, -1, 1, d)
    ada = _ada(c, w_ada, b_ada).reshape(depth, b, 1, 6 * d)
    x2 = x.reshape(t, d)
    h = _modulate(x2, ada[0], s)

    for l in range(depth):
        lambda_init = 0.8 - 0.6 * float(np.exp(-0.3 * l))
        ada_l = ada[l]
        n_bc = w_in.shape[2] - 3 * QKV_A
        proj_a = _matmul(h, w_in16, l, 0, 3 * QKV_A).reshape(b, s, 3 * QKV_A)
        proj_bc = _matmul(h, w_in16, l, 3 * QKV_A, n_bc).reshape(b, s, n_bc)
        gates = _gates(h, w_gate16, b_gate4, l)
        outs_a, lses_a = _mixer_a(proj_a)
        y_b = _mixer_b(proj_bc, lam_vecs[l], subln_g[l], lambda_init)
        y_c = _mixer_c(proj_bc, sinks[l])
        merged = _merge([o.reshape(t, -1) for o in outs_a], [v.reshape(t, -1) for v in lses_a],
                        y_b.reshape(t, -1), y_c.reshape(t, -1), gates, w_branch16, l)
        x2, route, counts = _wo_ln_route(merged, w_o16, l, x2, ada_l, ln_g[l, 0:1], ln_b[l, 0:1],
                                         w_r_hi[l], w_r_lo[l], b_r[l], s, alpha)
        starts_lanes, plan = _dispatch_plan(counts, 2 * t)
        pos = _positions(route, starts_lanes).reshape(-1)
        xs = _dispatch(x2, ada_l, pos, s)
        ys = _experts(xs, w1_16, w3_16, w2_16, l, plan)
        ada_next = ada[min(l + 1, depth - 1)]
        x2, h = _combine(ys, pos, route, x2, ada_l, ada_next, ln_g[l, 1:2], ln_b[l, 1:2], s, alpha)
    return x2.reshape(b, s, d)
```

```python
import functools

import numpy as np
import jax
import jax.numpy as jnp
from jax import lax
from jax.experimental import pallas as pl
from jax.experimental.pallas import tpu as pltpu

F32 = jnp.float32
BF16 = jnp.bfloat16
HIGHEST = lax.Precision.HIGHEST

BLOCK = 128
DILATED_GROUPS = ((128, 1), (512, 4), (2048, 16))
N_GROUPS_A = 3
HEADS_A = 4
HEAD_DIM_A = 128
GROUP_WIDTH_A = HEADS_A * HEAD_DIM_A
QKV_A = N_GROUPS_A * GROUP_WIDTH_A
HEADS_B = 4
HEAD_DIM_B = 64
HEADS_C = 8
KV_HEADS_C = 2
GQA_GROUP_C = HEADS_C // KV_HEADS_C
HEAD_DIM_C = 64
WINDOW_C = 128
BRANCH_WIDTH = 512
N_EXPERT_GROUPS = 4
EXPERTS_PER_GROUP = 8
N_EXPERTS = N_EXPERT_GROUPS * EXPERTS_PER_GROUP
LN_EPS = 1e-5
NEG = -1e30

LANES = 128
SUBLANES = 8
ROUTE_LANE0 = N_EXPERT_GROUPS
VMEM_LIMIT = 56 * 1024 * 1024

EXPERT_TILE = 512
ROW_CHUNK = 16
ROW_UNROLL = 4


def _alibi_slopes(n):
    return [float(v) for v in 2.0 ** (-8.0 * np.arange(1, n + 1, dtype=np.float32) / n)]


def _params(*sem):
    return pltpu.CompilerParams(dimension_semantics=sem, vmem_limit_bytes=VMEM_LIMIT)


def _ada_kernel(c_ref, w_ref, b_ref, o_ref):
    c = c_ref[...]
    cond = c * jax.nn.sigmoid(c)
    o_ref[0] = jnp.dot(cond, w_ref[0], preferred_element_type=F32, precision=HIGHEST) + b_ref[0]


def _ada(c, w_ada, b_ada):
    depth, d, n = w_ada.shape
    bsz = c.shape[0]
    tn = 1024
    return pl.pallas_call(
        _ada_kernel,
        grid=(depth, n // tn),
        in_specs=[pl.BlockSpec((bsz, d), lambda l, j: (0, 0)),
                  pl.BlockSpec((1, d, tn), lambda l, j: (l, 0, j)),
                  pl.BlockSpec((1, 1, tn), lambda l, j: (l, 0, j))],
        out_specs=pl.BlockSpec((1, bsz, tn), lambda l, j: (l, 0, j)),
        out_shape=jax.ShapeDtypeStruct((depth, bsz, n), F32),
        compiler_params=_params("arbitrary", "arbitrary"),
        name="ada",
    )(c, w_ada, b_ada.reshape(depth, 1, n))


def _modulate_kernel(x_ref, sh_ref, sc_ref, o_ref):
    o_ref[...] = (x_ref[...] * (1.0 + sc_ref[0]) + sh_ref[0]).astype(o_ref.dtype)


def _modulate(x2, ada_l, seq, tm=512):
    t, d = x2.shape
    per_b = seq // tm
    return pl.pallas_call(
        _modulate_kernel,
        grid=(t // tm,),
        in_specs=[pl.BlockSpec((tm, d), lambda i: (i, 0)),
                  pl.BlockSpec((1, 1, d), lambda i: (i // per_b, 0, 0)),
                  pl.BlockSpec((1, 1, d), lambda i: (i // per_b, 0, 1))],
        out_specs=pl.BlockSpec((tm, d), lambda i: (i, 0)),
        out_shape=jax.ShapeDtypeStruct((t, d), BF16),
        compiler_params=_params("arbitrary"),
        name="modulate",
    )(x2, ada_l, ada_l)


def _mm_kernel(x_ref, w_ref, o_ref):
    o_ref[...] = jnp.dot(x_ref[...], w_ref[0], preferred_element_type=F32).astype(o_ref.dtype)


def _matmul(x2, w, layer, col0, n, tm=1024, tn=768):
    t, d = x2.shape
    tm = min(tm, t)
    c0 = col0 // tn
    assert col0 % tn == 0 and n % tn == 0
    return pl.pallas_call(
        _mm_kernel,
        grid=(t // tm, n // tn),
        in_specs=[pl.BlockSpec((tm, d), lambda i, j: (i, 0)),
                  pl.BlockSpec((1, d, tn), lambda i, j: (layer, 0, c0 + j))],
        out_specs=pl.BlockSpec((tm, tn), lambda i, j: (i, j)),
        out_shape=jax.ShapeDtypeStruct((t, n), BF16),
        compiler_params=_params("arbitrary", "arbitrary"),
        name="proj",
    )(x2, w)


def _proj_a_kernel(x_ref, w_ref, o0_ref, o1_ref, o2_ref, buf):
    g = pl.program_id(1) % N_GROUPS_A
    acc = jnp.dot(x_ref[...], w_ref[0], preferred_element_type=F32)
    tm = acc.shape[0]
    for gi, (o_ref, (_, r)) in enumerate(zip((o0_ref, o1_ref, o2_ref), DILATED_GROUPS)):
        @pl.when(g == gi)
        def _(o_ref=o_ref, r=r):
            if r == 1:
                o_ref[0, 0] = acc.astype(o_ref.dtype)
            else:
                for c in range(GROUP_WIDTH_A // LANES):
                    buf[c] = acc[:, c * LANES:(c + 1) * LANES]
                for rho in range(r):
                    for c in range(GROUP_WIDTH_A // LANES):
                        o_ref[0, rho, :, c * LANES:(c + 1) * LANES] = (
                            buf[c, pl.ds(rho, tm // r, stride=r), :].astype(o_ref.dtype))


def _proj_a(x2, w, layer, bsz, seq, tm=1024):
    t, d = x2.shape
    tm = min(tm, seq)
    per_b = seq // tm
    w_g = GROUP_WIDTH_A
    n_tiles = 3 * N_GROUPS_A

    def out_spec(g, r):
        return pl.BlockSpec((1, r, tm // r, w_g),
                            lambda i, j: (i // per_b, 0, i % per_b, jnp.maximum(j - g, 0) // N_GROUPS_A))

    return pl.pallas_call(
        _proj_a_kernel,
        grid=(t // tm, n_tiles),
        in_specs=[pl.BlockSpec((tm, d), lambda i, j: (i, 0)),
                  pl.BlockSpec((1, d, w_g), lambda i, j: (layer, 0, j))],
        out_specs=[out_spec(g, r) for g, (_, r) in enumerate(DILATED_GROUPS)],
        out_shape=[jax.ShapeDtypeStruct((bsz, r, seq // r, 3 * w_g), BF16) for _, r in DILATED_GROUPS],
        scratch_shapes=[pltpu.VMEM((w_g // LANES, tm, LANES), F32)],
        compiler_params=_params("arbitrary", "arbitrary"),
        name="proj_a",
    )(x2, w)


def _gate_kernel(x_ref, w_ref, b_ref, o_ref):
    acc = jnp.dot(x_ref[...], w_ref[0, 0], preferred_element_type=F32) + b_ref[0, 0]
    o_ref[...] = jax.nn.sigmoid(acc).astype(o_ref.dtype)


def _gates(x2, w_gate, b_gate, layer, tm=1024, tn=1024):
    t, d = x2.shape
    _, nbr, _, n = w_gate.shape
    tm = min(tm, t)
    per = n // tn
    return pl.pallas_call(
        _gate_kernel,
        grid=(t // tm, nbr * per),
        in_specs=[pl.BlockSpec((tm, d), lambda i, j: (i, 0)),
                  pl.BlockSpec((1, 1, d, tn), lambda i, j: (layer, j // per, 0, j % per)),
                  pl.BlockSpec((1, 1, 1, tn), lambda i, j: (layer, j // per, 0, j % per))],
        out_specs=pl.BlockSpec((tm, tn), lambda i, j: (i, j)),
        out_shape=jax.ShapeDtypeStruct((t, nbr * n), BF16),
        compiler_params=_params("arbitrary", "arbitrary"),
        name="gates",
    )(x2, w_gate, b_gate)


def _attn_a_kernel(q_ref, kp_ref, kc_ref, vp_ref, vc_ref, o_ref, lse_ref, kbuf, vbuf, *, slopes, dist_unit, nb):
    n = pl.program_id(1)
    kbuf[0:BLOCK, :] = kp_ref[0]
    kbuf[BLOCK:, :] = kc_ref[0]
    vbuf[0:BLOCK, :] = vp_ref[0]
    vbuf[BLOCK:, :] = vc_ref[0]
    row = lax.broadcasted_iota(jnp.int32, (BLOCK, 2 * BLOCK), 0)
    col = lax.broadcasted_iota(jnp.int32, (BLOCK, 2 * BLOCK), 1)
    dist = row + BLOCK - col
    band = (dist >= 0) & (dist <= BLOCK)
    first_lim = jnp.where(n > 0, 0, BLOCK)
    band_first = band & (col >= first_lim)
    distf = dist.astype(F32) * float(dist_unit)
    scale = HEAD_DIM_A ** -0.5
    for h in range(HEADS_A):
        hs = slice(h * HEAD_DIM_A, (h + 1) * HEAD_DIM_A)
        bias = -slopes[h] * distf
        for i in range(nb // BLOCK):
            rows = slice(i * BLOCK, (i + 1) * BLOCK)
            q = q_ref[0, rows, hs]
            k = kbuf[i * BLOCK:(i + 2) * BLOCK, hs]
            v = vbuf[i * BLOCK:(i + 2) * BLOCK, hs]
            s = lax.dot_general(q, k, (((1,), (1,)), ((), ())), preferred_element_type=F32) * scale + bias
            s = jnp.where(band_first if i == 0 else band, s, NEG)
            m = jnp.max(s, axis=-1, keepdims=True)
            p = jnp.exp(s - m)
            l = jnp.sum(p, axis=-1, keepdims=True)
            o = jnp.dot(p.astype(BF16), v, preferred_element_type=F32) / l
            o_ref[0, rows, hs] = o.astype(o_ref.dtype)
            lse_ref[0, rows, h:h + 1] = m + jnp.log(l)


def _attn_a_group(qkv, cols, slopes, dist_unit):
    nseq, l, _ = qkv.shape
    nb = min(512, l)
    sub = nb // BLOCK
    cq, ck, cv = cols
    w = GROUP_WIDTH_A
    kern = functools.partial(_attn_a_kernel, slopes=slopes, dist_unit=dist_unit, nb=nb)
    prev = lambda b, n: jnp.maximum(n * sub - 1, 0)
    return pl.pallas_call(
        kern,
        grid=(nseq, l // nb),
        in_specs=[pl.BlockSpec((1, nb, w), lambda b, n: (b, n, cq)),
                  pl.BlockSpec((1, BLOCK, w), lambda b, n: (b, prev(b, n), ck)),
                  pl.BlockSpec((1, nb, w), lambda b, n: (b, n, ck)),
                  pl.BlockSpec((1, BLOCK, w), lambda b, n: (b, prev(b, n), cv)),
                  pl.BlockSpec((1, nb, w), lambda b, n: (b, n, cv))],
        out_specs=[pl.BlockSpec((1, nb, w), lambda b, n: (b, n, 0)),
                   pl.BlockSpec((1, nb, HEADS_A), lambda b, n: (b, n, 0))],
        out_shape=[jax.ShapeDtypeStruct((nseq, l, w), BF16),
                   jax.ShapeDtypeStruct((nseq, l, HEADS_A), F32)],
        scratch_shapes=[pltpu.VMEM((BLOCK + nb, w), BF16), pltpu.VMEM((BLOCK + nb, w), BF16)],
        compiler_params=_params("arbitrary", "arbitrary"),
        name="attn_a",
    )(qkv, qkv, qkv, qkv, qkv)


def _mixer_a(qkv_groups):
    slopes = _alibi_slopes(N_GROUPS_A * HEADS_A)
    outs, lses = [], []
    for g, (window, r) in enumerate(DILATED_GROUPS):
        assert window // r == BLOCK
        b, _, l, w = qkv_groups[g].shape
        o, lse = _attn_a_group(qkv_groups[g].reshape(b * r, l, w), (0, 1, 2), slopes[g * HEADS_A:(g + 1) * HEADS_A], r)
        outs.append(o.reshape(b, r, l, GROUP_WIDTH_A))
        lses.append(lse.reshape(b, r, l, HEADS_A).transpose(0, 2, 1, 3).reshape(b * l * r, HEADS_A))
    return outs, lses


def _attn_b_kernel(q_ref, k_ref, v_ref, lv_ref, g_ref, o_ref, kt_buf, s_buf, m_buf, l_buf, acc_buf, *,
                   slopes, lambda_init, tq):
    h = pl.program_id(1)
    qi = pl.program_id(2)
    d = HEAD_DIM_B
    nk = kt_buf.shape[1]
    nt = (((1,), (1,)), ((), ()))

    @pl.when(qi == 0)
    def _():
        eye = (lax.broadcasted_iota(jnp.int32, (d, d), 0) == lax.broadcasted_iota(jnp.int32, (d, d), 1)).astype(BF16)

        def transpose(c, carry):
            k = k_ref[0, pl.ds(pl.multiple_of(c * tq, tq), tq), :]
            for j in range(2):
                kt = lax.dot_general(eye, k[:, j * d:(j + 1) * d], nt, preferred_element_type=F32)
                kt_buf[j, c] = kt.astype(BF16)
            return carry

        lax.fori_loop(0, nk, transpose, 0)

    slope = jnp.float32(slopes[0])
    for j in range(1, HEADS_B):
        slope = jnp.where(h == j, jnp.float32(slopes[j]), slope)
    q = q_ref[0]
    qs = [(q[:, j * d:(j + 1) * d].astype(F32) * (d ** -0.5)).astype(BF16) for j in range(2)]
    colf = lax.broadcasted_iota(jnp.int32, (1, tq), 1).astype(F32)
    nl = tq // LANES

    def lane_fold(x, op, acc):
        for c in range(nl):
            acc = op(acc, x[:, c * LANES:(c + 1) * LANES])
        return acc

    m_buf[...] = jnp.full(m_buf.shape, NEG, F32)

    def scores(kb, masked):
        bias = slope * (colf + ((kb - qi) * tq).astype(F32))
        for j in range(2):
            s = jnp.dot(qs[j], kt_buf[j, kb], preferred_element_type=F32) + bias
            if masked:
                row = lax.broadcasted_iota(jnp.int32, (tq, tq), 0)
                col = lax.broadcasted_iota(jnp.int32, (tq, tq), 1)
                s = jnp.where(col <= row, s, NEG)
            s_buf[j, kb] = s
            m_buf[j] = lane_fold(s, jnp.maximum, m_buf[j])

    def scores_body(kb, carry):
        scores(kb, False)
        return carry

    lax.fori_loop(0, qi, scores_body, 0)
    scores(qi, True)
    ms = [jnp.max(m_buf[j], axis=-1, keepdims=True) for j in range(2)]

    l_buf[...] = jnp.zeros_like(l_buf)
    acc_buf[...] = jnp.zeros_like(acc_buf)

    def weights(kb, carry):
        v = v_ref[0, pl.ds(pl.multiple_of(kb * tq, tq), tq), :]
        for j in range(2):
            p = jnp.exp(s_buf[j, kb] - ms[j])
            l_buf[j] = lane_fold(p, jnp.add, l_buf[j])
            acc_buf[j] = acc_buf[j] + jnp.dot(p.astype(BF16), v, preferred_element_type=F32)
        return carry

    lax.fori_loop(0, qi + 1, weights, 0)
    l0, l1 = (jnp.sum(l_buf[j], axis=-1, keepdims=True) for j in range(2))
    a0, a1 = acc_buf[0], acc_buf[1]
    lv = lv_ref[...]
    lam = (jnp.exp(jnp.sum(lv[0:1] * lv[1:2], keepdims=True)) - jnp.exp(jnp.sum(lv[2:3] * lv[3:4], keepdims=True))
           + lambda_init)
    o = a0 / l0 - lam * (a1 / l1)
    o = o * lax.rsqrt(jnp.mean(o * o, axis=-1, keepdims=True) + LN_EPS)
    o = o * g_ref[...] * (1.0 - lambda_init)
    o_ref[0] = o.astype(o_ref.dtype)


def _mixer_b(proj_bc, lam_vecs, subln_g, lambda_init, tq=512):
    b, s, _ = proj_bc.shape
    w = 2 * HEAD_DIM_B
    nk = s // tq
    kern = functools.partial(_attn_b_kernel, slopes=_alibi_slopes(HEADS_B), lambda_init=lambda_init, tq=tq)
    return pl.pallas_call(
        kern,
        grid=(b, HEADS_B, nk),
        in_specs=[pl.BlockSpec((1, tq, w), lambda bi, h, i: (bi, i, h)),
                  pl.BlockSpec((1, s, w), lambda bi, h, i: (bi, 0, HEADS_B + h)),
                  pl.BlockSpec((1, s, w), lambda bi, h, i: (bi, 0, 2 * HEADS_B + h)),
                  pl.BlockSpec(lam_vecs.shape, lambda bi, h, i: (0, 0)),
                  pl.BlockSpec((1, w), lambda bi, h, i: (0, 0))],
        out_specs=pl.BlockSpec((1, tq, w), lambda bi, h, i: (bi, i, h)),
        out_shape=jax.ShapeDtypeStruct((b, s, HEADS_B * w), BF16),
        scratch_shapes=[pltpu.VMEM((2, nk, HEAD_DIM_B, tq), BF16), pltpu.VMEM((2, nk, tq, tq), F32),
                        pltpu.VMEM((2, tq, LANES), F32), pltpu.VMEM((2, tq, LANES), F32),
                        pltpu.VMEM((2, tq, w), F32)],
        compiler_params=_params("arbitrary", "arbitrary", "arbitrary"),
        name="attn_b",
    )(proj_bc, proj_bc, proj_bc, lam_vecs, subln_g.reshape(1, w))


def _attn_c_kernel(sink_ref, q_ref, kp_ref, kc_ref, vp_ref, vc_ref, o_ref, kbuf, vbuf, *, slopes, nb):
    n = pl.program_id(1)
    d = HEAD_DIM_C
    half = lax.broadcasted_iota(jnp.int32, (1, 2 * d), 1) < d

    for j in range(KV_HEADS_C):
        for src_p, src_c, dst in ((kp_ref, kc_ref, kbuf), (vp_ref, vc_ref, vbuf)):
            xp = src_p[0, :, j * d:(j + 1) * d]
            xc = src_c[0, :, j * d:(j + 1) * d]
            dst[j, 0:BLOCK, :] = jnp.concatenate([xp, xp], axis=1)
            dst[j, BLOCK:, :] = jnp.concatenate([xc, xc], axis=1)

    row = lax.broadcasted_iota(jnp.int32, (BLOCK, 2 * BLOCK), 0)
    col = lax.broadcasted_iota(jnp.int32, (BLOCK, 2 * BLOCK), 1)
    dist = row + BLOCK - col
    band = (dist >= 0) & (dist <= WINDOW_C - 1)
    first_lim = jnp.where(n > 0, 0, BLOCK)
    band_first = band & (col >= first_lim)
    distf = dist.astype(F32)
    scale = d ** -0.5
    zero = jnp.zeros((), BF16)
    for hp in range(HEADS_C // 2):
        kv = (2 * hp) // GQA_GROUP_C
        for i in range(nb // BLOCK):
            rows = slice(i * BLOCK, (i + 1) * BLOCK)
            qp = q_ref[0, rows, hp * 2 * d:(hp + 1) * 2 * d]
            k = kbuf[kv, i * BLOCK:(i + 2) * BLOCK, :]
            v = vbuf[kv, i * BLOCK:(i + 2) * BLOCK, :]
            outs = []
            for e in range(2):
                h = 2 * hp + e
                q = jnp.where(half if e == 0 else ~half, qp, zero)
                s = lax.dot_general(q, k, (((1,), (1,)), ((), ())), preferred_element_type=F32) * scale - slopes[h] * distf
                s = jnp.where(band_first if i == 0 else band, s, NEG)
                sink = sink_ref[h]
                m = jnp.maximum(jnp.max(s, axis=-1, keepdims=True), sink)
                p = jnp.exp(s - m)
                denom = jnp.sum(p, axis=-1, keepdims=True) + jnp.exp(sink - m)
                outs.append(jnp.dot(p.astype(BF16), v, preferred_element_type=F32) / denom)
            o_ref[0, rows, hp * 2 * d:(hp + 1) * 2 * d] = jnp.where(half, outs[0], outs[1]).astype(o_ref.dtype)


def _mixer_c(proj_bc, sinks):
    b, s, _ = proj_bc.shape
    nb = min(512, s)
    sub = nb // BLOCK
    wq = HEADS_C * HEAD_DIM_C
    wkv = KV_HEADS_C * HEAD_DIM_C
    cq, ck, cv = 1536 // wq, 2048 // wkv, 2176 // wkv
    kern = functools.partial(_attn_c_kernel, slopes=_alibi_slopes(HEADS_C), nb=nb)
    prev = lambda n: jnp.maximum(n * sub - 1, 0)
    return pl.pallas_call(
        kern,
        grid=(b, s // nb),
        in_specs=[pl.BlockSpec(memory_space=pltpu.SMEM),
                  pl.BlockSpec((1, nb, wq), lambda bi, n: (bi, n, cq)),
                  pl.BlockSpec((1, BLOCK, wkv), lambda bi, n: (bi, prev(n), ck)),
                  pl.BlockSpec((1, nb, wkv), lambda bi, n: (bi, n, ck)),
                  pl.BlockSpec((1, BLOCK, wkv), lambda bi, n: (bi, prev(n), cv)),
                  pl.BlockSpec((1, nb, wkv), lambda bi, n: (bi, n, cv))],
        out_specs=pl.BlockSpec((1, nb, wq), lambda bi, n: (bi, n, 0)),
        out_shape=jax.ShapeDtypeStruct((b, s, wq), BF16),
        scratch_shapes=[pltpu.VMEM((KV_HEADS_C, BLOCK + nb, 2 * HEAD_DIM_C), BF16),
                        pltpu.VMEM((KV_HEADS_C, BLOCK + nb, 2 * HEAD_DIM_C), BF16)],
        compiler_params=_params("arbitrary", "arbitrary"),
        name="attn_c",
    )(sinks, proj_bc, proj_bc, proj_bc, proj_bc, proj_bc)


def _merge_kernel(o1_ref, o2_ref, o3_ref, l1_ref, l2_ref, l3_ref, yb_ref, yc_ref, g_ref, wb_ref, out_ref, obuf):
    la, lb, lc = l1_ref[...], l2_ref[...], l3_ref[...]
    m = jnp.maximum(jnp.maximum(la, lb), lc)
    ea, eb, ec = jnp.exp(la - m), jnp.exp(lb - m), jnp.exp(lc - m)
    den = ea + eb + ec
    wts = (ea / den, eb / den, ec / den)
    tm = out_ref.shape[0]
    for gi, (o_ref, (_, r)) in enumerate(zip((o1_ref, o2_ref, o3_ref), DILATED_GROUPS)):
        for rho in range(r):
            for h in range(HEADS_A):
                hs = slice(h * HEAD_DIM_A, (h + 1) * HEAD_DIM_A)
                if r == 1:
                    obuf[gi, h] = o_ref[0, 0, :, hs].astype(F32)
                else:
                    obuf[gi, h, pl.ds(rho, tm // r, stride=r), :] = o_ref[0, rho, :, hs].astype(F32)
    parts = []
    for h in range(HEADS_A):
        y = wts[0][:, h:h + 1] * obuf[0, h] + wts[1][:, h:h + 1] * obuf[1, h] + wts[2][:, h:h + 1] * obuf[2, h]
        parts.append(y.astype(BF16))
    ya = jnp.concatenate(parts, axis=1)
    dm = out_ref.shape[1]
    acc = g_ref[:, 0:dm].astype(F32) * jnp.dot(ya, wb_ref[0, 0], preferred_element_type=F32)
    acc = acc + g_ref[:, dm:2 * dm].astype(F32) * jnp.dot(yb_ref[...], wb_ref[0, 1], preferred_element_type=F32)
    acc = acc + g_ref[:, 2 * dm:3 * dm].astype(F32) * jnp.dot(yc_ref[...], wb_ref[0, 2], preferred_element_type=F32)
    out_ref[...] = acc.astype(out_ref.dtype)


def _merge(outs_a, lses_a, y_b, y_c, gates, w_branch, layer, seq, tm=512):
    t = y_b.shape[0]
    dm = w_branch.shape[3]
    per_b = seq // tm
    wide = lambda w: pl.BlockSpec((tm, w), lambda i: (i, 0))
    group = lambda r: pl.BlockSpec((1, r, tm // r, GROUP_WIDTH_A), lambda i: (i // per_b, 0, i % per_b, 0))
    return pl.pallas_call(
        _merge_kernel,
        grid=(t // tm,),
        in_specs=[group(r) for _, r in DILATED_GROUPS] + [wide(HEADS_A)] * 3 + [wide(BRANCH_WIDTH)] * 2
                 + [wide(3 * dm)] + [pl.BlockSpec((1,) + w_branch.shape[1:], lambda i: (layer, 0, 0, 0))],
        out_specs=wide(dm),
        out_shape=jax.ShapeDtypeStruct((t, dm), BF16),
        scratch_shapes=[pltpu.VMEM((N_GROUPS_A, HEADS_A, tm, HEAD_DIM_A), F32)],
        compiler_params=_params("arbitrary"),
        name="merge",
    )(*outs_a, *lses_a, y_b, y_c, gates, w_branch)


def _layer_norm(z, g, b):
    mu = jnp.mean(z, axis=-1, keepdims=True)
    zc = z - mu
    var = jnp.mean(zc * zc, axis=-1, keepdims=True)
    return zc * lax.rsqrt(var + LN_EPS) * g + b


def _wo_ln_route_kernel(a_ref, wo_ref, x_ref, g1_ref, sh_ref, sc_ref, lng_ref, lnb_ref, whi_ref, wlo_ref, br_ref,
                        xo_ref, route_ref, cnt_ref, run_ref, mix_buf, hhi_buf, hlo_buf, *, alpha, tm):
    @pl.when(pl.program_id(0) == 0)
    def _():
        run_ref[...] = jnp.zeros_like(run_ref)

    mix_buf[...] = jnp.dot(a_ref[...], wo_ref[0], preferred_element_type=F32)

    def rows(r, carry):
        rs = pl.ds(pl.multiple_of(r * ROW_CHUNK, ROW_CHUNK), ROW_CHUNK)
        xn = _layer_norm(alpha * x_ref[rs, :] + g1_ref[0] * mix_buf[rs, :], lng_ref[...], lnb_ref[...])
        xo_ref[rs, :] = xn
        h = xn * (1.0 + sc_ref[0]) + sh_ref[0]
        hi = h.astype(BF16)
        hhi_buf[rs, :] = hi
        hlo_buf[rs, :] = (h - hi.astype(F32)).astype(BF16)
        return carry

    lax.fori_loop(0, tm // ROW_CHUNK, rows, 0, unroll=ROW_UNROLL)

    hhi = hhi_buf[...]
    logits = (jnp.dot(hhi, whi_ref[...], preferred_element_type=F32)
              + jnp.dot(hlo_buf[...], whi_ref[...], preferred_element_type=F32)
              + jnp.dot(hhi, wlo_ref[...], preferred_element_type=F32) + br_ref[...])
    lane = lax.broadcasted_iota(jnp.int32, (tm, LANES), 1).astype(F32)
    far = float(4 * LANES)
    gl = jnp.where(lane < N_EXPERT_GROUPS, logits, NEG)
    gm = jnp.max(gl, axis=-1, keepdims=True)
    g_val = 1.0 / jnp.sum(jnp.exp(gl - gm), axis=-1, keepdims=True)
    g_idx = jnp.min(jnp.where(gl == gm, lane, far), axis=-1, keepdims=True)
    lo = ROUTE_LANE0 + EXPERTS_PER_GROUP * g_idx
    fl = jnp.where((lane >= lo) & (lane < lo + EXPERTS_PER_GROUP), logits, NEG)
    m1 = jnp.max(fl, axis=-1, keepdims=True)
    i1 = jnp.min(jnp.where(fl == m1, lane, far), axis=-1, keepdims=True)
    fl2 = jnp.where(lane == i1, NEG, fl)
    m2 = jnp.max(fl2, axis=-1, keepdims=True)
    i2 = jnp.min(jnp.where(fl2 == m2, lane, far), axis=-1, keepdims=True)
    e = jnp.exp(m2 - m1)
    w1 = g_val / (1.0 + e)
    w2 = g_val * e / (1.0 + e)

    oh1 = lane == i1
    oh2 = lane == i2
    c = jnp.where(oh1 | oh2, 1.0, 0.0)
    r_i = lax.broadcasted_iota(jnp.int32, (tm, tm), 0)
    c_i = lax.broadcasted_iota(jnp.int32, (tm, tm), 1)
    before = jnp.where(c_i < r_i, 1.0, 0.0).astype(BF16)
    tot = jnp.dot(before, c.astype(BF16), preferred_element_type=F32) + run_ref[...]
    rank1 = jnp.sum(jnp.where(oh1, tot, 0.0), axis=-1, keepdims=True)
    rank2 = jnp.sum(jnp.where(oh2, tot, 0.0), axis=-1, keepdims=True)
    run_ref[...] = run_ref[...] + jnp.sum(c, axis=0, keepdims=True)
    cnt_ref[...] = run_ref[...]

    l8 = lax.broadcasted_iota(jnp.int32, (tm, 8), 1)
    e1 = i1 - ROUTE_LANE0
    e2 = i2 - ROUTE_LANE0
    route = jnp.where(l8 == 0, w1, jnp.where(l8 == 1, w2, jnp.where(l8 == 2, e1, jnp.where(l8 == 3, e2,
            jnp.where(l8 == 4, rank1, rank2)))))
    route_ref[...] = route


def _wo_ln_route(merged, w_o, layer, x2, ada_l, ln_g, ln_b, w_r_hi, w_r_lo, b_r, seq, alpha, tm=256):
    t, d = x2.shape
    per_b = seq // tm
    vec = lambda k: pl.BlockSpec((1, 1, d), lambda i: (i // per_b, 0, k))
    row = lambda w: pl.BlockSpec((tm, w), lambda i: (i, 0))
    full = lambda a: pl.BlockSpec(a.shape, lambda i: (0,) * a.ndim)
    kern = functools.partial(_wo_ln_route_kernel, alpha=alpha, tm=tm)
    return pl.pallas_call(
        kern,
        grid=(t // tm,),
        in_specs=[row(d), pl.BlockSpec((1, d, d), lambda i: (layer, 0, 0)), row(d), vec(2), vec(3), vec(4),
                  full(ln_g), full(ln_b), full(w_r_hi), full(w_r_lo), full(b_r)],
        out_specs=[row(d), row(8), pl.BlockSpec((1, LANES), lambda i: (0, 0))],
        out_shape=[jax.ShapeDtypeStruct((t, d), F32), jax.ShapeDtypeStruct((t, 8), F32),
                   jax.ShapeDtypeStruct((1, LANES), F32)],
        scratch_shapes=[pltpu.VMEM((1, LANES), F32), pltpu.VMEM((tm, d), F32), pltpu.VMEM((tm, d), BF16),
                        pltpu.VMEM((tm, d), BF16)],
        compiler_params=_params("arbitrary"),
        name="wo_ln_route",
    )(merged, w_o, x2, ada_l, ada_l, ada_l, ln_g, ln_b, w_r_hi, w_r_lo, b_r)


def _row_copy(src, i, dst, j, sem):
    return pltpu.make_async_copy(src.at[pl.ds(i, 1), :], dst.at[pl.ds(j, 1), :], sem)


def _dispatch_kernel(p0_ref, p1_ref, x_ref, sh_ref, sc_ref, xs_hbm, hbuf, sem, *, tc):
    step = pl.program_id(0)
    slot = step % 2
    stage = hbuf.at[slot]

    def rows(r, carry):
        rs = pl.ds(pl.multiple_of(r * ROW_CHUNK, ROW_CHUNK), ROW_CHUNK)
        stage[rs, :] = x_ref[rs, :] * (1.0 + sc_ref[0]) + sh_ref[0]
        return carry

    lax.fori_loop(0, tc // ROW_CHUNK, rows, 0, unroll=ROW_UNROLL)

    def issue(k, c):
        base = pl.multiple_of(k * SUBLANES, SUBLANES)
        for r in range(SUBLANES):
            _row_copy(stage, base + r, xs_hbm, p0_ref[base + r], sem.at[slot]).start()
            _row_copy(stage, base + r, xs_hbm, p1_ref[base + r], sem.at[slot]).start()
        return c

    lax.fori_loop(0, tc // SUBLANES, issue, 0)

    def drain(which):
        def body(i, c):
            _row_copy(hbuf.at[which], 0, xs_hbm, 0, sem.at[which]).wait()
            return c

        lax.fori_loop(0, 2 * tc, body, 0, unroll=8)

    @pl.when(step > 0)
    def _():
        drain(1 - slot)

    @pl.when(step == pl.num_programs(0) - 1)
    def _():
        drain(slot)


def _dispatch(x2, ada_l, pos0, pos1, seq, tc=256):
    t, d = x2.shape
    per_b = seq // tc
    vec = lambda k: pl.BlockSpec((1, 1, d), lambda i: (i // per_b, 0, k))
    return pl.pallas_call(
        functools.partial(_dispatch_kernel, tc=tc),
        grid=(t // tc,),
        in_specs=[pl.BlockSpec((tc,), lambda i: (i,), memory_space=pltpu.SMEM),
                  pl.BlockSpec((tc,), lambda i: (i,), memory_space=pltpu.SMEM),
                  pl.BlockSpec((tc, d), lambda i: (i, 0)), vec(3), vec(4)],
        out_specs=pl.BlockSpec(memory_space=pl.ANY),
        out_shape=jax.ShapeDtypeStruct((2 * t, d), F32),
        scratch_shapes=[pltpu.VMEM((2, tc, d), F32), pltpu.SemaphoreType.DMA((2,))],
        compiler_params=_params("arbitrary"),
        name="dispatch",
    )(pos0, pos1, x2, ada_l, ada_l)


def _expert_kernel(it_ref, ie_ref, lo_ref, hi_ref, first_ref, nu_ref, x_ref, w1_ref, w3_ref, w2_ref, y_ref, *, tile):
    i = pl.program_id(0)

    @pl.when(i < nu_ref[0])
    def _():
        x = x_ref[...].astype(BF16)
        a = jnp.dot(x, w1_ref[0, 0], preferred_element_type=F32)
        b = jnp.dot(x, w3_ref[0, 0], preferred_element_type=F32)
        he = (a * jax.nn.sigmoid(a) * b).astype(BF16)
        y = jnp.dot(he, w2_ref[0, 0], preferred_element_type=F32)
        row = it_ref[i] * tile + lax.broadcasted_iota(jnp.int32, (tile, 1), 0)
        mine = (row >= lo_ref[i]) & (row < hi_ref[i])

        @pl.when(first_ref[i] == 1)
        def _():
            y_ref[...] = jnp.where(mine, y, 0.0)

        @pl.when(first_ref[i] == 0)
        def _():
            y_ref[...] = jnp.where(mine, y, y_ref[...])


def _experts(xs, w1, w3, w2, layer, plan):
    n_rows, d = xs.shape
    de = w1.shape[3]
    tile = EXPERT_TILE
    n_items = plan[0].shape[0]
    row_map = lambda i, it, ie, lo, hi, fi, nu: (it[i], 0)
    w_map = lambda i, it, ie, lo, hi, fi, nu: (layer, ie[i], 0, 0)
    grid_spec = pltpu.PrefetchScalarGridSpec(
        num_scalar_prefetch=6,
        grid=(n_items,),
        in_specs=[pl.BlockSpec((tile, d), row_map),
                  pl.BlockSpec((1, 1, d, de), w_map),
                  pl.BlockSpec((1, 1, d, de), w_map),
                  pl.BlockSpec((1, 1, de, d), w_map)],
        out_specs=pl.BlockSpec((tile, d), row_map),
    )
    return pl.pallas_call(
        functools.partial(_expert_kernel, tile=tile),
        grid_spec=grid_spec,
        out_shape=jax.ShapeDtypeStruct((n_rows, d), F32),
        compiler_params=_params("arbitrary"),
        name="experts",
    )(*plan, xs, w1, w3, w2)


def _combine_kernel(p0_ref, p1_ref, p0n_ref, p1n_ref, ys_hbm, route_ref, x_ref, g2_ref, lng_ref, lnb_ref, sh_ref, sc_ref,
                    xo_ref, h_ref, buf, sem, *, alpha, tc):
    step = pl.program_id(0)
    slot = step % 2

    def gather(pa_ref, pb_ref, s):
        def issue(k, c):
            base = pl.multiple_of(k * SUBLANES, SUBLANES)
            for r in range(SUBLANES):
                _row_copy(ys_hbm, pa_ref[base + r], buf.at[s, 0], base + r, sem.at[s]).start()
                _row_copy(ys_hbm, pb_ref[base + r], buf.at[s, 1], base + r, sem.at[s]).start()
            return c

        lax.fori_loop(0, tc // SUBLANES, issue, 0)

    @pl.when(step == 0)
    def _():
        gather(p0_ref, p1_ref, 0)

    @pl.when(step + 1 < pl.num_programs(0))
    def _():
        gather(p0n_ref, p1n_ref, 1 - slot)

    def drain(i, c):
        _row_copy(ys_hbm, 0, buf.at[slot, 0], 0, sem.at[slot]).wait()
        return c

    lax.fori_loop(0, 2 * tc, drain, 0, unroll=8)

    def rows(r, carry):
        rs = pl.ds(pl.multiple_of(r * ROW_CHUNK, ROW_CHUNK), ROW_CHUNK)
        w = route_ref[rs, :]
        ffn = w[:, 0:1] * buf[slot, 0, rs, :] + w[:, 1:2] * buf[slot, 1, rs, :]
        xn = _layer_norm(alpha * x_ref[rs, :] + g2_ref[0] * ffn, lng_ref[...], lnb_ref[...])
        xo_ref[rs, :] = xn
        h_ref[rs, :] = (xn * (1.0 + sc_ref[0]) + sh_ref[0]).astype(h_ref.dtype)
        return carry

    lax.fori_loop(0, tc // ROW_CHUNK, rows, 0, unroll=ROW_UNROLL)


def _combine(ys, pos0, pos1, route, x2, ada_l, ada_next, ln_g, ln_b, seq, alpha, tc=256):
    t, d = x2.shape
    per_b = seq // tc
    n = t // tc
    vec = lambda k: pl.BlockSpec((1, 1, d), lambda i: (i // per_b, 0, k))
    row = lambda w: pl.BlockSpec((tc, w), lambda i: (i, 0))
    full = lambda a: pl.BlockSpec(a.shape, lambda i: (0,) * a.ndim)
    kern = functools.partial(_combine_kernel, alpha=alpha, tc=tc)
    return pl.pallas_call(
        kern,
        grid=(n,),
        in_specs=[pl.BlockSpec((tc,), lambda i: (i,), memory_space=pltpu.SMEM),
                  pl.BlockSpec((tc,), lambda i: (i,), memory_space=pltpu.SMEM),
                  pl.BlockSpec((tc,), lambda i: (jnp.minimum(i + 1, n - 1),), memory_space=pltpu.SMEM),
                  pl.BlockSpec((tc,), lambda i: (jnp.minimum(i + 1, n - 1),), memory_space=pltpu.SMEM),
                  pl.BlockSpec(memory_space=pl.ANY),
                  row(8), row(d), vec(5), full(ln_g), full(ln_b), vec(0), vec(1)],
        out_specs=[row(d), row(d)],
        out_shape=[jax.ShapeDtypeStruct((t, d), F32), jax.ShapeDtypeStruct((t, d), BF16)],
        scratch_shapes=[pltpu.VMEM((2, 2, tc, d), F32), pltpu.SemaphoreType.DMA((2,))],
        compiler_params=_params("arbitrary"),
        name="combine",
    )(pos0, pos1, pos0, pos1, ys, route, x2, ada_l, ln_g, ln_b, ada_next, ada_next)


def _positions_kernel(route_ref, starts_ref, pos_ref):
    r = route_ref[...]
    tm = r.shape[0]
    li = lax.broadcasted_iota(jnp.int32, (tm, LANES), 1)
    lane = li.astype(F32)
    starts = starts_ref[...]

    def start_of(e):
        return jnp.sum(jnp.where(lane == e + ROUTE_LANE0, starts, 0.0), axis=-1, keepdims=True)

    p1 = start_of(r[:, 2:3]) + r[:, 4:5]
    p2 = start_of(r[:, 3:4]) + r[:, 5:6]
    cols = jnp.where(li == 0, p1, jnp.where(li == 1, p2, 0.0))
    pos_ref[...] = cols.T[0:8, :].astype(jnp.int32)


def _positions(route, starts, tm=2048):
    t = route.shape[0]
    tm = min(tm, t)
    return pl.pallas_call(
        _positions_kernel,
        grid=(t // tm,),
        in_specs=[pl.BlockSpec((tm, 8), lambda i: (i, 0)), pl.BlockSpec((1, LANES), lambda i: (0, 0))],
        out_specs=pl.BlockSpec((8, tm), lambda i: (0, i)),
        out_shape=jax.ShapeDtypeStruct((8, t), jnp.int32),
        compiler_params=_params("arbitrary"),
        name="positions",
    )(route, starts)


def _dispatch_plan(counts, n_rows):
    tile = EXPERT_TILE
    n_items = n_rows // tile + N_EXPERTS
    starts_lanes = jnp.cumsum(counts, axis=1) - counts
    cnt = counts[0, ROUTE_LANE0:ROUTE_LANE0 + N_EXPERTS].astype(jnp.int32)
    ends = jnp.cumsum(cnt)
    starts = ends - cnt
    first_tile = starts // tile
    n_e = jnp.where(cnt > 0, (ends - 1) // tile - first_tile + 1, 0)
    item_ends = jnp.cumsum(n_e)
    n_used = item_ends[-1]
    k = jnp.minimum(jnp.arange(n_items, dtype=jnp.int32), n_used - 1)
    item_expert = jnp.minimum(jnp.sum((item_ends[None, :] <= k[:, None]).astype(jnp.int32), axis=1), N_EXPERTS - 1)
    onehot = (item_expert[:, None] == jnp.arange(N_EXPERTS, dtype=jnp.int32)[None, :]).astype(jnp.int32)
    pick = lambda v: jnp.sum(onehot * v[None, :], axis=1)
    item_tile = pick(first_tile) + k - pick(item_ends - n_e)
    item_first = jnp.concatenate([jnp.ones((1,), jnp.int32), (item_tile[1:] != item_tile[:-1]).astype(jnp.int32)])
    plan = (item_tile, item_expert, pick(starts), pick(ends), item_first, n_used.reshape(1))
    return starts_lanes, tuple(p.astype(jnp.int32) for p in plan)


def kernel(x, c, w_ada, b_ada, w_in, w_gate, b_gate, w_branch, w_o, lam_vecs, subln_g, sinks, ln_g, ln_b,
           w_rg, b_rg, w_rf, b_rf, w1, w3, w2):
    b, s, d = x.shape
    depth = w_ada.shape[0]
    t = b * s
    alpha = (2 * depth) ** 0.25
    assert s % (16 * BLOCK) == 0 and d % LANES == 0

    w_in16 = w_in.astype(BF16)
    w_gate16 = w_gate.astype(BF16)
    w_branch16 = w_branch.astype(BF16)
    w_o16 = w_o.astype(BF16)
    w1_16, w3_16, w2_16 = w1.astype(BF16), w3.astype(BF16), w2.astype(BF16)
    pad = LANES - N_EXPERT_GROUPS - N_EXPERTS
    w_r = jnp.concatenate([w_rg, w_rf, jnp.zeros((depth, d, pad), F32)], axis=-1)
    w_r_hi = w_r.astype(BF16)
    w_r_lo = (w_r - w_r_hi.astype(F32)).astype(BF16)
    b_r = jnp.concatenate([b_rg, b_rf, jnp.zeros((depth, pad), F32)], axis=-1).reshape(depth, 1, LANES)

    b_gate4 = b_gate.reshape(depth, -1, 1, d)
    ada = _ada(c, w_ada, b_ada).reshape(depth, b, 1, 6 * d)
    x2 = x.reshape(t, d)
    h = _modulate(x2, ada[0], s)

    for l in range(depth):
        lambda_init = 0.8 - 0.6 * float(np.exp(-0.3 * l))
        ada_l = ada[l]
        n_bc = w_in.shape[2] - 3 * QKV_A
        qkv_groups = _proj_a(h, w_in16, l, b, s)
        proj_bc = _matmul(h, w_in16, l, 3 * QKV_A, n_bc).reshape(b, s, n_bc)
        gates = _gates(h, w_gate16, b_gate4, l)
        outs_a, lses_a = _mixer_a(qkv_groups)
        y_b = _mixer_b(proj_bc, lam_vecs[l], subln_g[l], lambda_init)
        y_c = _mixer_c(proj_bc, sinks[l])
        merged = _merge(outs_a, lses_a, y_b.reshape(t, -1), y_c.reshape(t, -1), gates, w_branch16, l, s)
        x2, route, counts = _wo_ln_route(merged, w_o16, l, x2, ada_l, ln_g[l, 0:1], ln_b[l, 0:1],
                                         w_r_hi[l], w_r_lo[l], b_r[l], s, alpha)
        starts_lanes, plan = _dispatch_plan(counts, 2 * t)
        pos = _positions(route, starts_lanes)
        pos0, pos1 = pos[0], pos[1]
        xs = _dispatch(x2, ada_l, pos0, pos1, s)
        ys = _experts(xs, w1_16, w3_16, w2_16, l, plan)
        ada_next = ada[min(l + 1, depth - 1)]
        x2, h = _combine(ys, pos0, pos1, route, x2, ada_l, ada_next, ln_g[l, 1:2], ln_b[l, 1:2], s, alpha)
    return x2.reshape(b, s, d)
```

```python
import functools

import numpy as np
import jax
import jax.numpy as jnp
from jax import lax
from jax.experimental import pallas as pl
from jax.experimental.pallas import tpu as pltpu

F32 = jnp.float32
BF16 = jnp.bfloat16
HIGHEST = lax.Precision.HIGHEST

BLOCK = 128
DILATED_GROUPS = ((128, 1), (512, 4), (2048, 16))
N_GROUPS_A = 3
HEADS_A = 4
HEAD_DIM_A = 128
GROUP_WIDTH_A = HEADS_A * HEAD_DIM_A
QKV_A = N_GROUPS_A * GROUP_WIDTH_A
HEADS_B = 4
HEAD_DIM_B = 64
HEADS_C = 8
KV_HEADS_C = 2
GQA_GROUP_C = HEADS_C // KV_HEADS_C
HEAD_DIM_C = 64
WINDOW_C = 128
BRANCH_WIDTH = 512
N_EXPERT_GROUPS = 4
EXPERTS_PER_GROUP = 8
N_EXPERTS = N_EXPERT_GROUPS * EXPERTS_PER_GROUP
LN_EPS = 1e-5
NEG = -1e30

LANES = 128
SUBLANES = 8
ROUTE_LANE0 = N_EXPERT_GROUPS
VMEM_LIMIT = 56 * 1024 * 1024
EXPERT_VMEM_LIMIT = 60 * 1024 * 1024

EXPERT_TILE = 512
ROW_CHUNK = 16
ROW_UNROLL = 4


def _alibi_slopes(n):
    return [float(v) for v in 2.0 ** (-8.0 * np.arange(1, n + 1, dtype=np.float32) / n)]


def _params(*sem):
    return pltpu.CompilerParams(dimension_semantics=sem, vmem_limit_bytes=VMEM_LIMIT)


def _ada_kernel(c_ref, w_ref, b_ref, o_ref):
    c = c_ref[...]
    cond = c * jax.nn.sigmoid(c)
    o_ref[0] = jnp.dot(cond, w_ref[0], preferred_element_type=F32, precision=HIGHEST) + b_ref[0]


def _ada(c, w_ada, b_ada):
    depth, d, n = w_ada.shape
    bsz = c.shape[0]
    tn = 1024
    return pl.pallas_call(
        _ada_kernel,
        grid=(depth, n // tn),
        in_specs=[pl.BlockSpec((bsz, d), lambda l, j: (0, 0)),
                  pl.BlockSpec((1, d, tn), lambda l, j: (l, 0, j)),
                  pl.BlockSpec((1, 1, tn), lambda l, j: (l, 0, j))],
        out_specs=pl.BlockSpec((1, bsz, tn), lambda l, j: (l, 0, j)),
        out_shape=jax.ShapeDtypeStruct((depth, bsz, n), F32),
        compiler_params=_params("arbitrary", "arbitrary"),
        name="ada",
    )(c, w_ada, b_ada.reshape(depth, 1, n))


def _modulate_kernel(x_ref, sh_ref, sc_ref, o_ref):
    o_ref[...] = (x_ref[...] * (1.0 + sc_ref[0]) + sh_ref[0]).astype(o_ref.dtype)


def _modulate(x2, ada_l, seq, tm=512):
    t, d = x2.shape
    per_b = seq // tm
    return pl.pallas_call(
        _modulate_kernel,
        grid=(t // tm,),
        in_specs=[pl.BlockSpec((tm, d), lambda i: (i, 0)),
                  pl.BlockSpec((1, 1, d), lambda i: (i // per_b, 0, 0)),
                  pl.BlockSpec((1, 1, d), lambda i: (i // per_b, 0, 1))],
        out_specs=pl.BlockSpec((tm, d), lambda i: (i, 0)),
        out_shape=jax.ShapeDtypeStruct((t, d), BF16),
        compiler_params=_params("arbitrary"),
        name="modulate",
    )(x2, ada_l, ada_l)


def _mm_kernel(x_ref, w_ref, o_ref):
    o_ref[...] = jnp.dot(x_ref[...], w_ref[0], preferred_element_type=F32).astype(o_ref.dtype)


def _matmul(x2, w, layer, col0, n, tm=1024, tn=768):
    t, d = x2.shape
    tm = min(tm, t)
    c0 = col0 // tn
    assert col0 % tn == 0 and n % tn == 0
    return pl.pallas_call(
        _mm_kernel,
        grid=(t // tm, n // tn),
        in_specs=[pl.BlockSpec((tm, d), lambda i, j: (i, 0)),
                  pl.BlockSpec((1, d, tn), lambda i, j: (layer, 0, c0 + j))],
        out_specs=pl.BlockSpec((tm, tn), lambda i, j: (i, j)),
        out_shape=jax.ShapeDtypeStruct((t, n), BF16),
        compiler_params=_params("arbitrary", "arbitrary"),
        name="proj",
    )(x2, w)


def _proj_a_kernel(x_ref, w_ref, o0_ref, o1_ref, o2_ref, buf):
    g = pl.program_id(1) % N_GROUPS_A
    acc = jnp.dot(x_ref[...], w_ref[0], preferred_element_type=F32)
    tm = acc.shape[0]
    for gi, (o_ref, (_, r)) in enumerate(zip((o0_ref, o1_ref, o2_ref), DILATED_GROUPS)):
        @pl.when(g == gi)
        def _(o_ref=o_ref, r=r):
            if r == 1:
                o_ref[0, 0] = acc.astype(o_ref.dtype)
            else:
                for c in range(GROUP_WIDTH_A // LANES):
                    buf[c] = acc[:, c * LANES:(c + 1) * LANES]
                for rho in range(r):
                    for c in range(GROUP_WIDTH_A // LANES):
                        o_ref[0, rho, :, c * LANES:(c + 1) * LANES] = (
                            buf[c, pl.ds(rho, tm // r, stride=r), :].astype(o_ref.dtype))


def _proj_a(x2, w, layer, bsz, seq, tm=1024):
    t, d = x2.shape
    tm = min(tm, seq)
    per_b = seq // tm
    w_g = GROUP_WIDTH_A
    n_tiles = 3 * N_GROUPS_A

    def out_spec(g, r):
        return pl.BlockSpec((1, r, tm // r, w_g),
                            lambda i, j: (i // per_b, 0, i % per_b, jnp.maximum(j - g, 0) // N_GROUPS_A))

    return pl.pallas_call(
        _proj_a_kernel,
        grid=(t // tm, n_tiles),
        in_specs=[pl.BlockSpec((tm, d), lambda i, j: (i, 0)),
                  pl.BlockSpec((1, d, w_g), lambda i, j: (layer, 0, j))],
        out_specs=[out_spec(g, r) for g, (_, r) in enumerate(DILATED_GROUPS)],
        out_shape=[jax.ShapeDtypeStruct((bsz, r, seq // r, 3 * w_g), BF16) for _, r in DILATED_GROUPS],
        scratch_shapes=[pltpu.VMEM((w_g // LANES, tm, LANES), F32)],
        compiler_params=_params("arbitrary", "arbitrary"),
        name="proj_a",
    )(x2, w)


def _gate_kernel(x_ref, w_ref, b_ref, o_ref):
    acc = jnp.dot(x_ref[...], w_ref[0, 0], preferred_element_type=F32) + b_ref[0, 0]
    o_ref[...] = jax.nn.sigmoid(acc).astype(o_ref.dtype)


def _gates(x2, w_gate, b_gate, layer, tm=1024, tn=1024):
    t, d = x2.shape
    _, nbr, _, n = w_gate.shape
    tm = min(tm, t)
    per = n // tn
    return pl.pallas_call(
        _gate_kernel,
        grid=(t // tm, nbr * per),
        in_specs=[pl.BlockSpec((tm, d), lambda i, j: (i, 0)),
                  pl.BlockSpec((1, 1, d, tn), lambda i, j: (layer, j // per, 0, j % per)),
                  pl.BlockSpec((1, 1, 1, tn), lambda i, j: (layer, j // per, 0, j % per))],
        out_specs=pl.BlockSpec((tm, tn), lambda i, j: (i, j)),
        out_shape=jax.ShapeDtypeStruct((t, nbr * n), BF16),
        compiler_params=_params("arbitrary", "arbitrary"),
        name="gates",
    )(x2, w_gate, b_gate)


def _attn_a_kernel(q_ref, kp_ref, kc_ref, vp_ref, vc_ref, o_ref, lse_ref, kbuf, vbuf, *, slopes, dist_unit, nb):
    n = pl.program_id(1)
    kbuf[0:BLOCK, :] = kp_ref[0]
    kbuf[BLOCK:, :] = kc_ref[0]
    vbuf[0:BLOCK, :] = vp_ref[0]
    vbuf[BLOCK:, :] = vc_ref[0]
    row = lax.broadcasted_iota(jnp.int32, (BLOCK, 2 * BLOCK), 0)
    col = lax.broadcasted_iota(jnp.int32, (BLOCK, 2 * BLOCK), 1)
    dist = row + BLOCK - col
    band = (dist >= 0) & (dist <= BLOCK)
    first_lim = jnp.where(n > 0, 0, BLOCK)
    band_first = band & (col >= first_lim)
    distf = dist.astype(F32) * float(dist_unit)
    scale = HEAD_DIM_A ** -0.5
    for h in range(HEADS_A):
        hs = slice(h * HEAD_DIM_A, (h + 1) * HEAD_DIM_A)
        bias = -slopes[h] * distf
        for i in range(nb // BLOCK):
            rows = slice(i * BLOCK, (i + 1) * BLOCK)
            q = q_ref[0, rows, hs]
            k = kbuf[i * BLOCK:(i + 2) * BLOCK, hs]
            v = vbuf[i * BLOCK:(i + 2) * BLOCK, hs]
            s = lax.dot_general(q, k, (((1,), (1,)), ((), ())), preferred_element_type=F32) * scale + bias
            s = jnp.where(band_first if i == 0 else band, s, NEG)
            m = jnp.max(s, axis=-1, keepdims=True)
            p = jnp.exp(s - m)
            l = jnp.sum(p, axis=-1, keepdims=True)
            o = jnp.dot(p.astype(BF16), v, preferred_element_type=F32) / l
            o_ref[0, rows, hs] = o.astype(o_ref.dtype)
            lse_ref[0, rows, h:h + 1] = m + jnp.log(l)


def _attn_a_group(qkv, cols, slopes, dist_unit):
    nseq, l, _ = qkv.shape
    nb = min(512, l)
    sub = nb // BLOCK
    cq, ck, cv = cols
    w = GROUP_WIDTH_A
    kern = functools.partial(_attn_a_kernel, slopes=slopes, dist_unit=dist_unit, nb=nb)
    prev = lambda b, n: jnp.maximum(n * sub - 1, 0)
    return pl.pallas_call(
        kern,
        grid=(nseq, l // nb),
        in_specs=[pl.BlockSpec((1, nb, w), lambda b, n: (b, n, cq)),
                  pl.BlockSpec((1, BLOCK, w), lambda b, n: (b, prev(b, n), ck)),
                  pl.BlockSpec((1, nb, w), lambda b, n: (b, n, ck)),
                  pl.BlockSpec((1, BLOCK, w), lambda b, n: (b, prev(b, n), cv)),
                  pl.BlockSpec((1, nb, w), lambda b, n: (b, n, cv))],
        out_specs=[pl.BlockSpec((1, nb, w), lambda b, n: (b, n, 0)),
                   pl.BlockSpec((1, nb, HEADS_A), lambda b, n: (b, n, 0))],
        out_shape=[jax.ShapeDtypeStruct((nseq, l, w), BF16),
                   jax.ShapeDtypeStruct((nseq, l, HEADS_A), F32)],
        scratch_shapes=[pltpu.VMEM((BLOCK + nb, w), BF16), pltpu.VMEM((BLOCK + nb, w), BF16)],
        compiler_params=_params("arbitrary", "arbitrary"),
        name="attn_a",
    )(qkv, qkv, qkv, qkv, qkv)


def _mixer_a(qkv_groups):
    slopes = _alibi_slopes(N_GROUPS_A * HEADS_A)
    outs, lses = [], []
    for g, (window, r) in enumerate(DILATED_GROUPS):
        assert window // r == BLOCK
        b, _, l, w = qkv_groups[g].shape
        o, lse = _attn_a_group(qkv_groups[g].reshape(b * r, l, w), (0, 1, 2), slopes[g * HEADS_A:(g + 1) * HEADS_A], r)
        outs.append(o.reshape(b, r, l, GROUP_WIDTH_A))
        lses.append(lse.reshape(b, r, l, HEADS_A).transpose(0, 2, 1, 3).reshape(b * l * r, HEADS_A))
    return outs, lses


def _attn_b_kernel(q_ref, k_ref, v_ref, lv_ref, g_ref, o_ref, kt_buf, s_buf, m_buf, acc_buf, *,
                   slopes, lambda_init, tq):
    h = pl.program_id(1)
    qi = pl.program_id(2)
    d = HEAD_DIM_B
    nk = kt_buf.shape[1]
    nt = (((1,), (1,)), ((), ()))

    @pl.when(qi == 0)
    def _():
        eye = (lax.broadcasted_iota(jnp.int32, (d, d), 0) == lax.broadcasted_iota(jnp.int32, (d, d), 1)).astype(BF16)

        def transpose(c, carry):
            k = k_ref[0, pl.ds(pl.multiple_of(c * tq, tq), tq), :]
            for j in range(2):
                kt = lax.dot_general(eye, k[:, j * d:(j + 1) * d], nt, preferred_element_type=F32)
                kt_buf[j, c] = kt.astype(BF16)
            return carry

        lax.fori_loop(0, nk, transpose, 0)

    slope = jnp.float32(slopes[0])
    for j in range(1, HEADS_B):
        slope = jnp.where(h == j, jnp.float32(slopes[j]), slope)
    q = q_ref[0]
    qs = [(q[:, j * d:(j + 1) * d].astype(F32) * (d ** -0.5)).astype(BF16) for j in range(2)]
    colf = lax.broadcasted_iota(jnp.int32, (1, tq), 1).astype(F32)
    nl = tq // LANES

    def lane_fold(x, op, acc):
        for c in range(nl):
            acc = op(acc, x[:, c * LANES:(c + 1) * LANES])
        return acc

    m_buf[...] = jnp.full(m_buf.shape, NEG, F32)

    def scores(kb, masked):
        bias = slope * (colf + ((kb - qi) * tq).astype(F32))
        for j in range(2):
            s = jnp.dot(qs[j], kt_buf[j, kb], preferred_element_type=F32) + bias
            if masked:
                row = lax.broadcasted_iota(jnp.int32, (tq, tq), 0)
                col = lax.broadcasted_iota(jnp.int32, (tq, tq), 1)
                s = jnp.where(col <= row, s, NEG)
            s_buf[j, kb] = s
            m_buf[j] = lane_fold(s, jnp.maximum, m_buf[j])

    def scores_body(kb, carry):
        scores(kb, False)
        return carry

    lax.fori_loop(0, qi, scores_body, 0)
    scores(qi, True)
    ms = [jnp.max(m_buf[j], axis=-1, keepdims=True) for j in range(2)]

    acc_buf[...] = jnp.zeros_like(acc_buf)
    ones_col = (lax.broadcasted_iota(jnp.int32, (tq, 2 * d), 1) == 0).astype(BF16)

    def weights(kb, carry):
        v = v_ref[0, pl.ds(pl.multiple_of(kb * tq, tq), tq), :]
        v1 = jnp.concatenate([v, ones_col], axis=1)
        for j in range(2):
            p = jnp.exp(s_buf[j, kb] - ms[j])
            acc_buf[j] = acc_buf[j] + jnp.dot(p.astype(BF16), v1, preferred_element_type=F32)
        return carry

    lax.fori_loop(0, qi + 1, weights, 0)
    a0, a1 = acc_buf[0, :, 0:2 * d], acc_buf[1, :, 0:2 * d]
    l0, l1 = acc_buf[0, :, 2 * d:2 * d + 1], acc_buf[1, :, 2 * d:2 * d + 1]
    lv = lv_ref[...]
    lam = (jnp.exp(jnp.sum(lv[0:1] * lv[1:2], keepdims=True)) - jnp.exp(jnp.sum(lv[2:3] * lv[3:4], keepdims=True))
           + lambda_init)
    o = a0 / l0 - lam * (a1 / l1)
    o = o * lax.rsqrt(jnp.mean(o * o, axis=-1, keepdims=True) + LN_EPS)
    o = o * g_ref[...] * (1.0 - lambda_init)
    o_ref[0] = o.astype(o_ref.dtype)


def _mixer_b(proj_bc, lam_vecs, subln_g, lambda_init, tq=512):
    b, s, _ = proj_bc.shape
    w = 2 * HEAD_DIM_B
    nk = s // tq
    kern = functools.partial(_attn_b_kernel, slopes=_alibi_slopes(HEADS_B), lambda_init=lambda_init, tq=tq)
    return pl.pallas_call(
        kern,
        grid=(b, HEADS_B, nk),
        in_specs=[pl.BlockSpec((1, tq, w), lambda bi, h, i: (bi, i, h)),
                  pl.BlockSpec((1, s, w), lambda bi, h, i: (bi, 0, HEADS_B + h)),
                  pl.BlockSpec((1, s, w), lambda bi, h, i: (bi, 0, 2 * HEADS_B + h)),
                  pl.BlockSpec(lam_vecs.shape, lambda bi, h, i: (0, 0)),
                  pl.BlockSpec((1, w), lambda bi, h, i: (0, 0))],
        out_specs=pl.BlockSpec((1, tq, w), lambda bi, h, i: (bi, i, h)),
        out_shape=jax.ShapeDtypeStruct((b, s, HEADS_B * w), BF16),
        scratch_shapes=[pltpu.VMEM((2, nk, HEAD_DIM_B, tq), BF16), pltpu.VMEM((2, nk, tq, tq), F32),
                        pltpu.VMEM((2, tq, LANES), F32), pltpu.VMEM((2, tq, 2 * w), F32)],
        compiler_params=_params("arbitrary", "arbitrary", "arbitrary"),
        name="attn_b",
    )(proj_bc, proj_bc, proj_bc, lam_vecs, subln_g.reshape(1, w))


def _attn_c_kernel(sink_ref, q_ref, kp_ref, kc_ref, vp_ref, vc_ref, o_ref, kbuf, vbuf, *, slopes, nb):
    n = pl.program_id(1)
    d = HEAD_DIM_C
    half = lax.broadcasted_iota(jnp.int32, (1, 2 * d), 1) < d

    for j in range(KV_HEADS_C):
        for src_p, src_c, dst in ((kp_ref, kc_ref, kbuf), (vp_ref, vc_ref, vbuf)):
            xp = src_p[0, :, j * d:(j + 1) * d]
            xc = src_c[0, :, j * d:(j + 1) * d]
            dst[j, 0:BLOCK, :] = jnp.concatenate([xp, xp], axis=1)
            dst[j, BLOCK:, :] = jnp.concatenate([xc, xc], axis=1)

    row = lax.broadcasted_iota(jnp.int32, (BLOCK, 2 * BLOCK), 0)
    col = lax.broadcasted_iota(jnp.int32, (BLOCK, 2 * BLOCK), 1)
    dist = row + BLOCK - col
    band = (dist >= 0) & (dist <= WINDOW_C - 1)
    first_lim = jnp.where(n > 0, 0, BLOCK)
    band_first = band & (col >= first_lim)
    distf = dist.astype(F32)
    scale = d ** -0.5
    zero = jnp.zeros((), BF16)
    for hp in range(HEADS_C // 2):
        kv = (2 * hp) // GQA_GROUP_C
        for i in range(nb // BLOCK):
            rows = slice(i * BLOCK, (i + 1) * BLOCK)
            qp = q_ref[0, rows, hp * 2 * d:(hp + 1) * 2 * d]
            k = kbuf[kv, i * BLOCK:(i + 2) * BLOCK, :]
            v = vbuf[kv, i * BLOCK:(i + 2) * BLOCK, :]
            outs = []
            for e in range(2):
                h = 2 * hp + e
                q = jnp.where(half if e == 0 else ~half, qp, zero)
                s = lax.dot_general(q, k, (((1,), (1,)), ((), ())), preferred_element_type=F32) * scale - slopes[h] * distf
                s = jnp.where(band_first if i == 0 else band, s, NEG)
                sink = sink_ref[h]
                m = jnp.maximum(jnp.max(s, axis=-1, keepdims=True), sink)
                p = jnp.exp(s - m)
                denom = jnp.sum(p, axis=-1, keepdims=True) + jnp.exp(sink - m)
                outs.append(jnp.dot(p.astype(BF16), v, preferred_element_type=F32) / denom)
            o_ref[0, rows, hp * 2 * d:(hp + 1) * 2 * d] = jnp.where(half, outs[0], outs[1]).astype(o_ref.dtype)


def _mixer_c(proj_bc, sinks):
    b, s, _ = proj_bc.shape
    nb = min(512, s)
    sub = nb // BLOCK
    wq = HEADS_C * HEAD_DIM_C
    wkv = KV_HEADS_C * HEAD_DIM_C
    cq, ck, cv = 1536 // wq, 2048 // wkv, 2176 // wkv
    kern = functools.partial(_attn_c_kernel, slopes=_alibi_slopes(HEADS_C), nb=nb)
    prev = lambda n: jnp.maximum(n * sub - 1, 0)
    return pl.pallas_call(
        kern,
        grid=(b, s // nb),
        in_specs=[pl.BlockSpec(memory_space=pltpu.SMEM),
                  pl.BlockSpec((1, nb, wq), lambda bi, n: (bi, n, cq)),
                  pl.BlockSpec((1, BLOCK, wkv), lambda bi, n: (bi, prev(n), ck)),
                  pl.BlockSpec((1, nb, wkv), lambda bi, n: (bi, n, ck)),
                  pl.BlockSpec((1, BLOCK, wkv), lambda bi, n: (bi, prev(n), cv)),
                  pl.BlockSpec((1, nb, wkv), lambda bi, n: (bi, n, cv))],
        out_specs=pl.BlockSpec((1, nb, wq), lambda bi, n: (bi, n, 0)),
        out_shape=jax.ShapeDtypeStruct((b, s, wq), BF16),
        scratch_shapes=[pltpu.VMEM((KV_HEADS_C, BLOCK + nb, 2 * HEAD_DIM_C), BF16),
                        pltpu.VMEM((KV_HEADS_C, BLOCK + nb, 2 * HEAD_DIM_C), BF16)],
        compiler_params=_params("arbitrary", "arbitrary"),
        name="attn_c",
    )(sinks, proj_bc, proj_bc, proj_bc, proj_bc, proj_bc)


def _merge_kernel(o1_ref, o2_ref, o3_ref, l1_ref, l2_ref, l3_ref, yb_ref, yc_ref, g_ref, wb_ref, out_ref, obuf):
    la, lb, lc = l1_ref[...], l2_ref[...], l3_ref[...]
    m = jnp.maximum(jnp.maximum(la, lb), lc)
    ea, eb, ec = jnp.exp(la - m), jnp.exp(lb - m), jnp.exp(lc - m)
    den = ea + eb + ec
    wts = (ea / den, eb / den, ec / den)
    tm = out_ref.shape[0]
    for gi, (o_ref, (_, r)) in enumerate(zip((o1_ref, o2_ref, o3_ref), DILATED_GROUPS)):
        for rho in range(r):
            for h in range(HEADS_A):
                hs = slice(h * HEAD_DIM_A, (h + 1) * HEAD_DIM_A)
                if r == 1:
                    obuf[gi, h] = o_ref[0, 0, :, hs].astype(F32)
                else:
                    obuf[gi, h, pl.ds(rho, tm // r, stride=r), :] = o_ref[0, rho, :, hs].astype(F32)
    parts = []
    for h in range(HEADS_A):
        y = wts[0][:, h:h + 1] * obuf[0, h] + wts[1][:, h:h + 1] * obuf[1, h] + wts[2][:, h:h + 1] * obuf[2, h]
        parts.append(y.astype(BF16))
    ya = jnp.concatenate(parts, axis=1)
    dm = out_ref.shape[1]
    acc = g_ref[:, 0:dm].astype(F32) * jnp.dot(ya, wb_ref[0, 0], preferred_element_type=F32)
    acc = acc + g_ref[:, dm:2 * dm].astype(F32) * jnp.dot(yb_ref[...], wb_ref[0, 1], preferred_element_type=F32)
    acc = acc + g_ref[:, 2 * dm:3 * dm].astype(F32) * jnp.dot(yc_ref[...], wb_ref[0, 2], preferred_element_type=F32)
    out_ref[...] = acc.astype(out_ref.dtype)


def _merge(outs_a, lses_a, y_b, y_c, gates, w_branch, layer, seq, tm=512):
    t = y_b.shape[0]
    dm = w_branch.shape[3]
    per_b = seq // tm
    wide = lambda w: pl.BlockSpec((tm, w), lambda i: (i, 0))
    group = lambda r: pl.BlockSpec((1, r, tm // r, GROUP_WIDTH_A), lambda i: (i // per_b, 0, i % per_b, 0))
    return pl.pallas_call(
        _merge_kernel,
        grid=(t // tm,),
        in_specs=[group(r) for _, r in DILATED_GROUPS] + [wide(HEADS_A)] * 3 + [wide(BRANCH_WIDTH)] * 2
                 + [wide(3 * dm)] + [pl.BlockSpec((1,) + w_branch.shape[1:], lambda i: (layer, 0, 0, 0))],
        out_specs=wide(dm),
        out_shape=jax.ShapeDtypeStruct((t, dm), BF16),
        scratch_shapes=[pltpu.VMEM((N_GROUPS_A, HEADS_A, tm, HEAD_DIM_A), F32)],
        compiler_params=_params("arbitrary"),
        name="merge",
    )(*outs_a, *lses_a, y_b, y_c, gates, w_branch)


def _layer_norm(z, g, b):
    mu = jnp.mean(z, axis=-1, keepdims=True)
    zc = z - mu
    var = jnp.mean(zc * zc, axis=-1, keepdims=True)
    return zc * lax.rsqrt(var + LN_EPS) * g + b


def _wo_ln_route_kernel(a_ref, wo_ref, x_ref, g1_ref, sh_ref, sc_ref, lng_ref, lnb_ref, whi_ref, wlo_ref, br_ref,
                        xo_ref, route_ref, cnt_ref, run_ref, mix_buf, hhi_buf, hlo_buf, *, alpha, tm):
    @pl.when(pl.program_id(0) == 0)
    def _():
        run_ref[...] = jnp.zeros_like(run_ref)

    mix_buf[...] = jnp.dot(a_ref[...], wo_ref[0], preferred_element_type=F32)

    def rows(r, carry):
        rs = pl.ds(pl.multiple_of(r * ROW_CHUNK, ROW_CHUNK), ROW_CHUNK)
        xn = _layer_norm(alpha * x_ref[rs, :] + g1_ref[0] * mix_buf[rs, :], lng_ref[...], lnb_ref[...])
        xo_ref[rs, :] = xn
        h = xn * (1.0 + sc_ref[0]) + sh_ref[0]
        hi = h.astype(BF16)
        hhi_buf[rs, :] = hi
        hlo_buf[rs, :] = (h - hi.astype(F32)).astype(BF16)
        return carry

    lax.fori_loop(0, tm // ROW_CHUNK, rows, 0, unroll=ROW_UNROLL)

    hhi = hhi_buf[...]
    logits = (jnp.dot(hhi, whi_ref[...], preferred_element_type=F32)
              + jnp.dot(hlo_buf[...], whi_ref[...], preferred_element_type=F32)
              + jnp.dot(hhi, wlo_ref[...], preferred_element_type=F32) + br_ref[...])
    lane = lax.broadcasted_iota(jnp.int32, (tm, LANES), 1).astype(F32)
    far = float(4 * LANES)
    gl = jnp.where(lane < N_EXPERT_GROUPS, logits, NEG)
    gm = jnp.max(gl, axis=-1, keepdims=True)
    g_val = 1.0 / jnp.sum(jnp.exp(gl - gm), axis=-1, keepdims=True)
    g_idx = jnp.min(jnp.where(gl == gm, lane, far), axis=-1, keepdims=True)
    lo = ROUTE_LANE0 + EXPERTS_PER_GROUP * g_idx
    fl = jnp.where((lane >= lo) & (lane < lo + EXPERTS_PER_GROUP), logits, NEG)
    m1 = jnp.max(fl, axis=-1, keepdims=True)
    i1 = jnp.min(jnp.where(fl == m1, lane, far), axis=-1, keepdims=True)
    fl2 = jnp.where(lane == i1, NEG, fl)
    m2 = jnp.max(fl2, axis=-1, keepdims=True)
    i2 = jnp.min(jnp.where(fl2 == m2, lane, far), axis=-1, keepdims=True)
    e = jnp.exp(m2 - m1)
    w1 = g_val / (1.0 + e)
    w2 = g_val * e / (1.0 + e)

    oh1 = lane == i1
    oh2 = lane == i2
    c = jnp.where(oh1 | oh2, 1.0, 0.0)
    r_i = lax.broadcasted_iota(jnp.int32, (tm, tm), 0)
    c_i = lax.broadcasted_iota(jnp.int32, (tm, tm), 1)
    before = jnp.where(c_i < r_i, 1.0, 0.0).astype(BF16)
    tot = jnp.dot(before, c.astype(BF16), preferred_element_type=F32) + run_ref[...]
    rank1 = jnp.sum(jnp.where(oh1, tot, 0.0), axis=-1, keepdims=True)
    rank2 = jnp.sum(jnp.where(oh2, tot, 0.0), axis=-1, keepdims=True)
    run_ref[...] = run_ref[...] + jnp.sum(c, axis=0, keepdims=True)
    cnt_ref[...] = run_ref[...]

    l8 = lax.broadcasted_iota(jnp.int32, (tm, 8), 1)
    e1 = i1 - ROUTE_LANE0
    e2 = i2 - ROUTE_LANE0
    route = jnp.where(l8 == 0, w1, jnp.where(l8 == 1, w2, jnp.where(l8 == 2, e1, jnp.where(l8 == 3, e2,
            jnp.where(l8 == 4, rank1, rank2)))))
    route_ref[...] = route


def _wo_ln_route(merged, w_o, layer, x2, ada_l, ln_g, ln_b, w_r_hi, w_r_lo, b_r, seq, alpha, tm=512):
    t, d = x2.shape
    per_b = seq // tm
    vec = lambda k: pl.BlockSpec((1, 1, d), lambda i: (i // per_b, 0, k))
    row = lambda w: pl.BlockSpec((tm, w), lambda i: (i, 0))
    full = lambda a: pl.BlockSpec(a.shape, lambda i: (0,) * a.ndim)
    kern = functools.partial(_wo_ln_route_kernel, alpha=alpha, tm=tm)
    return pl.pallas_call(
        kern,
        grid=(t // tm,),
        in_specs=[row(d), pl.BlockSpec((1, d, d), lambda i: (layer, 0, 0)), row(d), vec(2), vec(3), vec(4),
                  full(ln_g), full(ln_b), full(w_r_hi), full(w_r_lo), full(b_r)],
        out_specs=[row(d), row(8), pl.BlockSpec((1, LANES), lambda i: (0, 0))],
        out_shape=[jax.ShapeDtypeStruct((t, d), F32), jax.ShapeDtypeStruct((t, 8), F32),
                   jax.ShapeDtypeStruct((1, LANES), F32)],
        scratch_shapes=[pltpu.VMEM((1, LANES), F32), pltpu.VMEM((tm, d), F32), pltpu.VMEM((tm, d), BF16),
                        pltpu.VMEM((tm, d), BF16)],
        compiler_params=_params("arbitrary"),
        name="wo_ln_route",
    )(merged, w_o, x2, ada_l, ada_l, ada_l, ln_g, ln_b, w_r_hi, w_r_lo, b_r)


def _row_copy(src, i, dst, j, sem):
    return pltpu.make_async_copy(src.at[pl.ds(i, 1), :], dst.at[pl.ds(j, 1), :], sem)


def _dispatch_kernel(p0_ref, p1_ref, x_ref, sh_ref, sc_ref, xs_hbm, hbuf, sem, *, tc):
    step = pl.program_id(0)
    slot = step % 2
    stage = hbuf.at[slot]

    def rows(r, carry):
        rs = pl.ds(pl.multiple_of(r * ROW_CHUNK, ROW_CHUNK), ROW_CHUNK)
        stage[rs, :] = x_ref[rs, :] * (1.0 + sc_ref[0]) + sh_ref[0]
        return carry

    lax.fori_loop(0, tc // ROW_CHUNK, rows, 0, unroll=ROW_UNROLL)

    def issue(k, c):
        base = pl.multiple_of(k * SUBLANES, SUBLANES)
        for r in range(SUBLANES):
            _row_copy(stage, base + r, xs_hbm, p0_ref[base + r], sem.at[slot]).start()
            _row_copy(stage, base + r, xs_hbm, p1_ref[base + r], sem.at[slot]).start()
        return c

    lax.fori_loop(0, tc // SUBLANES, issue, 0)

    def drain(which):
        def body(i, c):
            _row_copy(hbuf.at[which], 0, xs_hbm, 0, sem.at[which]).wait()
            return c

        lax.fori_loop(0, 2 * tc, body, 0, unroll=8)

    @pl.when(step > 0)
    def _():
        drain(1 - slot)

    @pl.when(step == pl.num_programs(0) - 1)
    def _():
        drain(slot)


def _dispatch(x2, ada_l, pos0, pos1, seq, tc=256):
    t, d = x2.shape
    per_b = seq // tc
    vec = lambda k: pl.BlockSpec((1, 1, d), lambda i: (i // per_b, 0, k))
    return pl.pallas_call(
        functools.partial(_dispatch_kernel, tc=tc),
        grid=(t // tc,),
        in_specs=[pl.BlockSpec((tc,), lambda i: (i,), memory_space=pltpu.SMEM),
                  pl.BlockSpec((tc,), lambda i: (i,), memory_space=pltpu.SMEM),
                  pl.BlockSpec((tc, d), lambda i: (i, 0)), vec(3), vec(4)],
        out_specs=pl.BlockSpec(memory_space=pl.ANY),
        out_shape=jax.ShapeDtypeStruct((2 * t, d), F32),
        scratch_shapes=[pltpu.VMEM((2, tc, d), F32), pltpu.SemaphoreType.DMA((2,))],
        compiler_params=_params("arbitrary"),
        name="dispatch",
    )(pos0, pos1, x2, ada_l, ada_l)


def _expert_kernel(it_ref, ie_ref, lo_ref, hi_ref, first_ref, nu_ref, x_ref, w1_ref, w3_ref, w2_ref, y_ref,
                   w1b, w3b, w2b, *, tile):
    i = pl.program_id(0)

    @pl.when((i == 0) | (ie_ref[i] != ie_ref[jnp.maximum(i - 1, 0)]))
    def _():
        w1b[...] = w1_ref[0, 0].astype(BF16)
        w3b[...] = w3_ref[0, 0].astype(BF16)
        w2b[...] = w2_ref[0, 0].astype(BF16)

    @pl.when(i < nu_ref[0])
    def _():
        x = x_ref[...].astype(BF16)
        a = jnp.dot(x, w1b[...], preferred_element_type=F32)
        b = jnp.dot(x, w3b[...], preferred_element_type=F32)
        he = (a * jax.nn.sigmoid(a) * b).astype(BF16)
        y = jnp.dot(he, w2b[...], preferred_element_type=F32)
        row = it_ref[i] * tile + lax.broadcasted_iota(jnp.int32, (tile, 1), 0)
        mine = (row >= lo_ref[i]) & (row < hi_ref[i])

        @pl.when(first_ref[i] == 1)
        def _():
            y_ref[...] = jnp.where(mine, y, 0.0)

        @pl.when(first_ref[i] == 0)
        def _():
            y_ref[...] = jnp.where(mine, y, y_ref[...])


def _experts(xs, w1, w3, w2, layer, plan):
    n_rows, d = xs.shape
    de = w1.shape[3]
    tile = EXPERT_TILE
    n_items = plan[0].shape[0]
    row_map = lambda i, it, ie, lo, hi, fi, nu: (it[i], 0)
    w_map = lambda i, it, ie, lo, hi, fi, nu: (layer, ie[i], 0, 0)
    grid_spec = pltpu.PrefetchScalarGridSpec(
        num_scalar_prefetch=6,
        grid=(n_items,),
        in_specs=[pl.BlockSpec((tile, d), row_map),
                  pl.BlockSpec((1, 1, d, de), w_map),
                  pl.BlockSpec((1, 1, d, de), w_map),
                  pl.BlockSpec((1, 1, de, d), w_map)],
        out_specs=pl.BlockSpec((tile, d), row_map),
        scratch_shapes=[pltpu.VMEM((d, de), BF16), pltpu.VMEM((d, de), BF16), pltpu.VMEM((de, d), BF16)],
    )
    return pl.pallas_call(
        functools.partial(_expert_kernel, tile=tile),
        grid_spec=grid_spec,
        out_shape=jax.ShapeDtypeStruct((n_rows, d), F32),
        compiler_params=pltpu.CompilerParams(dimension_semantics=("arbitrary",), vmem_limit_bytes=EXPERT_VMEM_LIMIT),
        name="experts",
    )(*plan, xs, w1, w3, w2)


def _combine_kernel(p0_ref, p1_ref, p0n_ref, p1n_ref, ys_hbm, route_ref, x_ref, g2_ref, lng_ref, lnb_ref, sh_ref, sc_ref,
                    xo_ref, h_ref, buf, sem, *, alpha, tc):
    step = pl.program_id(0)
    slot = step % 2

    def issue_rows(pa_ref, pb_ref, s, base, n):
        for r in range(n):
            _row_copy(ys_hbm, pa_ref[base + r], buf.at[s, 0], base + r, sem.at[s]).start()
            _row_copy(ys_hbm, pb_ref[base + r], buf.at[s, 1], base + r, sem.at[s]).start()

    @pl.when(step == 0)
    def _():
        def issue(k, c):
            issue_rows(p0_ref, p1_ref, 0, pl.multiple_of(k * SUBLANES, SUBLANES), SUBLANES)
            return c

        lax.fori_loop(0, tc // SUBLANES, issue, 0)

    def drain(i, c):
        _row_copy(ys_hbm, 0, buf.at[slot, 0], 0, sem.at[slot]).wait()
        return c

    lax.fori_loop(0, 2 * tc, drain, 0, unroll=8)

    def ln_rows(r):
        rs = pl.ds(pl.multiple_of(r * ROW_CHUNK, ROW_CHUNK), ROW_CHUNK)
        w = route_ref[rs, :]
        ffn = w[:, 0:1] * buf[slot, 0, rs, :] + w[:, 1:2] * buf[slot, 1, rs, :]
        xn = _layer_norm(alpha * x_ref[rs, :] + g2_ref[0] * ffn, lng_ref[...], lnb_ref[...])
        xo_ref[rs, :] = xn
        h_ref[rs, :] = (xn * (1.0 + sc_ref[0]) + sh_ref[0]).astype(h_ref.dtype)

    group = ROW_CHUNK * ROW_UNROLL

    @pl.when(step + 1 < pl.num_programs(0))
    def _():
        def body(g, carry):
            issue_rows(p0n_ref, p1n_ref, 1 - slot, pl.multiple_of(g * group, group), group)
            for u in range(ROW_UNROLL):
                ln_rows(g * ROW_UNROLL + u)
            return carry

        lax.fori_loop(0, tc // group, body, 0)

    @pl.when(step + 1 == pl.num_programs(0))
    def _():
        def body(g, carry):
            for u in range(ROW_UNROLL):
                ln_rows(g * ROW_UNROLL + u)
            return carry

        lax.fori_loop(0, tc // group, body, 0)


def _combine(ys, pos0, pos1, route, x2, ada_l, ada_next, ln_g, ln_b, seq, alpha, tc=256):
    t, d = x2.shape
    per_b = seq // tc
    n = t // tc
    vec = lambda k: pl.BlockSpec((1, 1, d), lambda i: (i // per_b, 0, k))
    row = lambda w: pl.BlockSpec((tc, w), lambda i: (i, 0))
    full = lambda a: pl.BlockSpec(a.shape, lambda i: (0,) * a.ndim)
    kern = functools.partial(_combine_kernel, alpha=alpha, tc=tc)
    return pl.pallas_call(
        kern,
        grid=(n,),
        in_specs=[pl.BlockSpec((tc,), lambda i: (i,), memory_space=pltpu.SMEM),
                  pl.BlockSpec((tc,), lambda i: (i,), memory_space=pltpu.SMEM),
                  pl.BlockSpec((tc,), lambda i: (jnp.minimum(i + 1, n - 1),), memory_space=pltpu.SMEM),
                  pl.BlockSpec((tc,), lambda i: (jnp.minimum(i + 1, n - 1),), memory_space=pltpu.SMEM),
                  pl.BlockSpec(memory_space=pl.ANY),
                  row(8), row(d), vec(5), full(ln_g), full(ln_b), vec(0), vec(1)],
        out_specs=[row(d), row(d)],
        out_shape=[jax.ShapeDtypeStruct((t, d), F32), jax.ShapeDtypeStruct((t, d), BF16)],
        scratch_shapes=[pltpu.VMEM((2, 2, tc, d), F32), pltpu.SemaphoreType.DMA((2,))],
        compiler_params=_params("arbitrary"),
        name="combine",
    )(pos0, pos1, pos0, pos1, ys, route, x2, ada_l, ln_g, ln_b, ada_next, ada_next)


def _positions_kernel(route_ref, starts_ref, pos_ref):
    r = route_ref[...]
    tm = r.shape[0]
    li = lax.broadcasted_iota(jnp.int32, (tm, LANES), 1)
    lane = li.astype(F32)
    starts = starts_ref[...]

    def start_of(e):
        return jnp.sum(jnp.where(lane == e + ROUTE_LANE0, starts, 0.0), axis=-1, keepdims=True)

    p1 = start_of(r[:, 2:3]) + r[:, 4:5]
    p2 = start_of(r[:, 3:4]) + r[:, 5:6]
    cols = jnp.where(li == 0, p1, jnp.where(li == 1, p2, 0.0))
    pos_ref[...] = cols.T[0:8, :].astype(jnp.int32)


def _positions(route, starts, tm=2048):
    t = route.shape[0]
    tm = min(tm, t)
    return pl.pallas_call(
        _positions_kernel,
        grid=(t // tm,),
        in_specs=[pl.BlockSpec((tm, 8), lambda i: (i, 0)), pl.BlockSpec((1, LANES), lambda i: (0, 0))],
        out_specs=pl.BlockSpec((8, tm), lambda i: (0, i)),
        out_shape=jax.ShapeDtypeStruct((8, t), jnp.int32),
        compiler_params=_params("arbitrary"),
        name="positions",
    )(route, starts)


def _dispatch_plan(counts, n_rows):
    tile = EXPERT_TILE
    n_items = n_rows // tile + N_EXPERTS
    starts_lanes = jnp.cumsum(counts, axis=1) - counts
    cnt = counts[0, ROUTE_LANE0:ROUTE_LANE0 + N_EXPERTS].astype(jnp.int32)
    ends = jnp.cumsum(cnt)
    starts = ends - cnt
    first_tile = starts // tile
    n_e = jnp.where(cnt > 0, (ends - 1) // tile - first_tile + 1, 0)
    item_ends = jnp.cumsum(n_e)
    n_used = item_ends[-1]
    k = jnp.minimum(jnp.arange(n_items, dtype=jnp.int32), n_used - 1)
    item_expert = jnp.minimum(jnp.sum((item_ends[None, :] <= k[:, None]).astype(jnp.int32), axis=1), N_EXPERTS - 1)
    onehot = (item_expert[:, None] == jnp.arange(N_EXPERTS, dtype=jnp.int32)[None, :]).astype(jnp.int32)
    pick = lambda v: jnp.sum(onehot * v[None, :], axis=1)
    item_tile = pick(first_tile) + k - pick(item_ends - n_e)
    item_first = jnp.concatenate([jnp.ones((1,), jnp.int32), (item_tile[1:] != item_tile[:-1]).astype(jnp.int32)])
    plan = (item_tile, item_expert, pick(starts), pick(ends), item_first, n_used.reshape(1))
    return starts_lanes, tuple(p.astype(jnp.int32) for p in plan)


def kernel(x, c, w_ada, b_ada, w_in, w_gate, b_gate, w_branch, w_o, lam_vecs, subln_g, sinks, ln_g, ln_b,
           w_rg, b_rg, w_rf, b_rf, w1, w3, w2):
    b, s, d = x.shape
    depth = w_ada.shape[0]
    t = b * s
    alpha = (2 * depth) ** 0.25
    assert s % (16 * BLOCK) == 0 and d % LANES == 0

    w_in16 = w_in.astype(BF16)
    w_gate16 = w_gate.astype(BF16)
    w_branch16 = w_branch.astype(BF16)
    w_o16 = w_o.astype(BF16)
    pad = LANES - N_EXPERT_GROUPS - N_EXPERTS
    w_r = jnp.concatenate([w_rg, w_rf, jnp.zeros((depth, d, pad), F32)], axis=-1)
    w_r_hi = w_r.astype(BF16)
    w_r_lo = (w_r - w_r_hi.astype(F32)).astype(BF16)
    b_r = jnp.concatenate([b_rg, b_rf, jnp.zeros((depth, pad), F32)], axis=-1).reshape(depth, 1, LANES)

    b_gate4 = b_gate.reshape(depth, -1, 1, d)
    ada = _ada(c, w_ada, b_ada).reshape(depth, b, 1, 6 * d)
    x2 = x.reshape(t, d)
    h = _modulate(x2, ada[0], s)

    for l in range(depth):
        lambda_init = 0.8 - 0.6 * float(np.exp(-0.3 * l))
        ada_l = ada[l]
        n_bc = w_in.shape[2] - 3 * QKV_A
        qkv_groups = _proj_a(h, w_in16, l, b, s)
        proj_bc = _matmul(h, w_in16, l, 3 * QKV_A, n_bc).reshape(b, s, n_bc)
        gates = _gates(h, w_gate16, b_gate4, l)
        outs_a, lses_a = _mixer_a(qkv_groups)
        y_b = _mixer_b(proj_bc, lam_vecs[l], subln_g[l], lambda_init)
        y_c = _mixer_c(proj_bc, sinks[l])
        merged = _merge(outs_a, lses_a, y_b.reshape(t, -1), y_c.reshape(t, -1), gates, w_branch16, l, s)
        x2, route, counts = _wo_ln_route(merged, w_o16, l, x2, ada_l, ln_g[l, 0:1], ln_b[l, 0:1],
                                         w_r_hi[l], w_r_lo[l], b_r[l], s, alpha)
        starts_lanes, plan = _dispatch_plan(counts, 2 * t)
        pos = _positions(route, starts_lanes)
        pos0, pos1 = pos[0], pos[1]
        xs = _dispatch(x2, ada_l, pos0, pos1, s)
        ys = _experts(xs, w1, w3, w2, l, plan)
        ada_next = ada[min(l + 1, depth - 1)]
        x2, h = _combine(ys, pos0, pos1, route, x2, ada_l, ada_next, ln_g[l, 1:2], ln_b[l, 1:2], s, alpha)
    return x2.reshape(b, s, d)
```

```python
import functools

import numpy as np
import jax
import jax.numpy as jnp
from jax import lax
from jax.experimental import pallas as pl
from jax.experimental.pallas import tpu as pltpu

F32 = jnp.float32
BF16 = jnp.bfloat16
HIGHEST = lax.Precision.HIGHEST

BLOCK = 128
DILATED_GROUPS = ((128, 1), (512, 4), (2048, 16))
N_GROUPS_A = 3
HEADS_A = 4
HEAD_DIM_A = 128
GROUP_WIDTH_A = HEADS_A * HEAD_DIM_A
QKV_A = N_GROUPS_A * GROUP_WIDTH_A
HEADS_B = 4
HEAD_DIM_B = 64
HEADS_C = 8
KV_HEADS_C = 2
GQA_GROUP_C = HEADS_C // KV_HEADS_C
HEAD_DIM_C = 64
WINDOW_C = 128
BRANCH_WIDTH = 512
N_EXPERT_GROUPS = 4
EXPERTS_PER_GROUP = 8
N_EXPERTS = N_EXPERT_GROUPS * EXPERTS_PER_GROUP
LN_EPS = 1e-5
NEG = -1e30

LANES = 128
SUBLANES = 8
ROUTE_LANE0 = N_EXPERT_GROUPS
VMEM_LIMIT = 56 * 1024 * 1024
EXPERT_VMEM_LIMIT = 60 * 1024 * 1024

EXPERT_TILE = 512
ROW_CHUNK = 16
ROW_UNROLL = 4


def _alibi_slopes(n):
    return [float(v) for v in 2.0 ** (-8.0 * np.arange(1, n + 1, dtype=np.float32) / n)]


def _params(*sem):
    return pltpu.CompilerParams(dimension_semantics=sem, vmem_limit_bytes=VMEM_LIMIT)


def _ada_kernel(c_ref, w_ref, b_ref, o_ref):
    c = c_ref[...]
    cond = c * jax.nn.sigmoid(c)
    o_ref[0] = jnp.dot(cond, w_ref[0], preferred_element_type=F32, precision=HIGHEST) + b_ref[0]


def _ada(c, w_ada, b_ada):
    depth, d, n = w_ada.shape
    bsz = c.shape[0]
    tn = 1024
    return pl.pallas_call(
        _ada_kernel,
        grid=(depth, n // tn),
        in_specs=[pl.BlockSpec((bsz, d), lambda l, j: (0, 0)),
                  pl.BlockSpec((1, d, tn), lambda l, j: (l, 0, j)),
                  pl.BlockSpec((1, 1, tn), lambda l, j: (l, 0, j))],
        out_specs=pl.BlockSpec((1, bsz, tn), lambda l, j: (l, 0, j)),
        out_shape=jax.ShapeDtypeStruct((depth, bsz, n), F32),
        compiler_params=_params("arbitrary", "arbitrary"),
        name="ada",
    )(c, w_ada, b_ada.reshape(depth, 1, n))


def _modulate_kernel(x_ref, sh_ref, sc_ref, o_ref):
    o_ref[...] = (x_ref[...] * (1.0 + sc_ref[0]) + sh_ref[0]).astype(o_ref.dtype)


def _modulate(x2, ada_l, seq, tm=512):
    t, d = x2.shape
    per_b = seq // tm
    return pl.pallas_call(
        _modulate_kernel,
        grid=(t // tm,),
        in_specs=[pl.BlockSpec((tm, d), lambda i: (i, 0)),
                  pl.BlockSpec((1, 1, d), lambda i: (i // per_b, 0, 0)),
                  pl.BlockSpec((1, 1, d), lambda i: (i // per_b, 0, 1))],
        out_specs=pl.BlockSpec((tm, d), lambda i: (i, 0)),
        out_shape=jax.ShapeDtypeStruct((t, d), BF16),
        compiler_params=_params("arbitrary"),
        name="modulate",
    )(x2, ada_l, ada_l)


def _mm_kernel(x_ref, w_ref, o_ref):
    o_ref[...] = jnp.dot(x_ref[...], w_ref[0], preferred_element_type=F32).astype(o_ref.dtype)


def _matmul(x2, w, layer, col0, n, tm=1024, tn=768):
    t, d = x2.shape
    tm = min(tm, t)
    c0 = col0 // tn
    assert col0 % tn == 0 and n % tn == 0
    return pl.pallas_call(
        _mm_kernel,
        grid=(t // tm, n // tn),
        in_specs=[pl.BlockSpec((tm, d), lambda i, j: (i, 0)),
                  pl.BlockSpec((1, d, tn), lambda i, j: (layer, 0, c0 + j))],
        out_specs=pl.BlockSpec((tm, tn), lambda i, j: (i, j)),
        out_shape=jax.ShapeDtypeStruct((t, n), BF16),
        compiler_params=_params("arbitrary", "arbitrary"),
        name="proj",
    )(x2, w)


def _proj_a_kernel(x_ref, w_ref, o0_ref, o1_ref, o2_ref, buf):
    g = pl.program_id(1) % N_GROUPS_A
    acc = jnp.dot(x_ref[...], w_ref[0], preferred_element_type=F32)
    tm = acc.shape[0]
    for gi, (o_ref, (_, r)) in enumerate(zip((o0_ref, o1_ref, o2_ref), DILATED_GROUPS)):
        @pl.when(g == gi)
        def _(o_ref=o_ref, r=r):
            if r == 1:
                o_ref[0, 0] = acc.astype(o_ref.dtype)
            else:
                for c in range(GROUP_WIDTH_A // LANES):
                    buf[c] = acc[:, c * LANES:(c + 1) * LANES]
                for rho in range(r):
                    for c in range(GROUP_WIDTH_A // LANES):
                        o_ref[0, rho, :, c * LANES:(c + 1) * LANES] = (
                            buf[c, pl.ds(rho, tm // r, stride=r), :].astype(o_ref.dtype))


def _proj_a(x2, w, layer, bsz, seq, tm=1024):
    t, d = x2.shape
    tm = min(tm, seq)
    per_b = seq // tm
    w_g = GROUP_WIDTH_A
    n_tiles = 3 * N_GROUPS_A

    def out_spec(g, r):
        return pl.BlockSpec((1, r, tm // r, w_g),
                            lambda i, j: (i // per_b, 0, i % per_b, jnp.maximum(j - g, 0) // N_GROUPS_A))

    return pl.pallas_call(
        _proj_a_kernel,
        grid=(t // tm, n_tiles),
        in_specs=[pl.BlockSpec((tm, d), lambda i, j: (i, 0)),
                  pl.BlockSpec((1, d, w_g), lambda i, j: (layer, 0, j))],
        out_specs=[out_spec(g, r) for g, (_, r) in enumerate(DILATED_GROUPS)],
        out_shape=[jax.ShapeDtypeStruct((bsz, r, seq // r, 3 * w_g), BF16) for _, r in DILATED_GROUPS],
        scratch_shapes=[pltpu.VMEM((w_g // LANES, tm, LANES), F32)],
        compiler_params=_params("arbitrary", "arbitrary"),
        name="proj_a",
    )(x2, w)


def _gate_kernel(x_ref, w_ref, b_ref, o_ref):
    acc = jnp.dot(x_ref[...], w_ref[0, 0], preferred_element_type=F32) + b_ref[0, 0]
    o_ref[...] = jax.nn.sigmoid(acc).astype(o_ref.dtype)


def _gates(x2, w_gate, b_gate, layer, tm=1024, tn=1024):
    t, d = x2.shape
    _, nbr, _, n = w_gate.shape
    tm = min(tm, t)
    per = n // tn
    return pl.pallas_call(
        _gate_kernel,
        grid=(t // tm, nbr * per),
        in_specs=[pl.BlockSpec((tm, d), lambda i, j: (i, 0)),
                  pl.BlockSpec((1, 1, d, tn), lambda i, j: (layer, j // per, 0, j % per)),
                  pl.BlockSpec((1, 1, 1, tn), lambda i, j: (layer, j // per, 0, j % per))],
        out_specs=pl.BlockSpec((tm, tn), lambda i, j: (i, j)),
        out_shape=jax.ShapeDtypeStruct((t, nbr * n), BF16),
        compiler_params=_params("arbitrary", "arbitrary"),
        name="gates",
    )(x2, w_gate, b_gate)


def _attn_a_kernel(q_ref, kp_ref, kc_ref, vp_ref, vc_ref, o_ref, lse_ref, kbuf, vbuf, *, slopes, dist_unit, nb):
    n = pl.program_id(1)
    kbuf[0:BLOCK, :] = kp_ref[0]
    kbuf[BLOCK:, :] = kc_ref[0]
    vbuf[0:BLOCK, :] = vp_ref[0]
    vbuf[BLOCK:, :] = vc_ref[0]
    row = lax.broadcasted_iota(jnp.int32, (BLOCK, 2 * BLOCK), 0)
    col = lax.broadcasted_iota(jnp.int32, (BLOCK, 2 * BLOCK), 1)
    dist = row + BLOCK - col
    band = (dist >= 0) & (dist <= BLOCK)
    first_lim = jnp.where(n > 0, 0, BLOCK)
    band_first = band & (col >= first_lim)
    distf = dist.astype(F32) * float(dist_unit)
    scale = HEAD_DIM_A ** -0.5
    for h in range(HEADS_A):
        hs = slice(h * HEAD_DIM_A, (h + 1) * HEAD_DIM_A)
        bias = -slopes[h] * distf
        for i in range(nb // BLOCK):
            rows = slice(i * BLOCK, (i + 1) * BLOCK)
            q = q_ref[0, rows, hs]
            k = kbuf[i * BLOCK:(i + 2) * BLOCK, hs]
            v = vbuf[i * BLOCK:(i + 2) * BLOCK, hs]
            s = lax.dot_general(q, k, (((1,), (1,)), ((), ())), preferred_element_type=F32) * scale + bias
            s = jnp.where(band_first if i == 0 else band, s, NEG)
            m = jnp.max(s, axis=-1, keepdims=True)
            p = jnp.exp(s - m)
            l = jnp.sum(p, axis=-1, keepdims=True)
            o = jnp.dot(p.astype(BF16), v, preferred_element_type=F32) / l
            o_ref[0, rows, hs] = o.astype(o_ref.dtype)
            lse_ref[0, rows, h:h + 1] = m + jnp.log(l)


def _attn_a_group(qkv, cols, slopes, dist_unit):
    nseq, l, _ = qkv.shape
    nb = min(1024, l)
    sub = nb // BLOCK
    cq, ck, cv = cols
    w = GROUP_WIDTH_A
    kern = functools.partial(_attn_a_kernel, slopes=slopes, dist_unit=dist_unit, nb=nb)
    prev = lambda b, n: jnp.maximum(n * sub - 1, 0)
    return pl.pallas_call(
        kern,
        grid=(nseq, l // nb),
        in_specs=[pl.BlockSpec((1, nb, w), lambda b, n: (b, n, cq)),
                  pl.BlockSpec((1, BLOCK, w), lambda b, n: (b, prev(b, n), ck)),
                  pl.BlockSpec((1, nb, w), lambda b, n: (b, n, ck)),
                  pl.BlockSpec((1, BLOCK, w), lambda b, n: (b, prev(b, n), cv)),
                  pl.BlockSpec((1, nb, w), lambda b, n: (b, n, cv))],
        out_specs=[pl.BlockSpec((1, nb, w), lambda b, n: (b, n, 0)),
                   pl.BlockSpec((1, nb, HEADS_A), lambda b, n: (b, n, 0))],
        out_shape=[jax.ShapeDtypeStruct((nseq, l, w), BF16),
                   jax.ShapeDtypeStruct((nseq, l, HEADS_A), F32)],
        scratch_shapes=[pltpu.VMEM((BLOCK + nb, w), BF16), pltpu.VMEM((BLOCK + nb, w), BF16)],
        compiler_params=_params("arbitrary", "arbitrary"),
        name="attn_a",
    )(qkv, qkv, qkv, qkv, qkv)


def _mixer_a(qkv_groups):
    slopes = _alibi_slopes(N_GROUPS_A * HEADS_A)
    outs, lses = [], []
    for g, (window, r) in enumerate(DILATED_GROUPS):
        assert window // r == BLOCK
        b, _, l, w = qkv_groups[g].shape
        o, lse = _attn_a_group(qkv_groups[g].reshape(b * r, l, w), (0, 1, 2), slopes[g * HEADS_A:(g + 1) * HEADS_A], r)
        outs.append(o.reshape(b, r, l, GROUP_WIDTH_A))
        lses.append(lse.reshape(b, r, l, HEADS_A).transpose(0, 2, 1, 3).reshape(b * l * r, HEADS_A))
    return outs, lses


def _attn_b_kernel(q_ref, k_ref, v_ref, lv_ref, g_ref, o_ref, kt_buf, s_buf, m_buf, acc_buf, *,
                   slopes, lambda_init, tq):
    h = pl.program_id(1)
    qi = pl.program_id(2)
    d = HEAD_DIM_B
    nk = kt_buf.shape[1]
    nt = (((1,), (1,)), ((), ()))

    @pl.when(qi == 0)
    def _():
        eye = (lax.broadcasted_iota(jnp.int32, (d, d), 0) == lax.broadcasted_iota(jnp.int32, (d, d), 1)).astype(BF16)

        def transpose(c, carry):
            k = k_ref[0, pl.ds(pl.multiple_of(c * tq, tq), tq), :]
            for j in range(2):
                kt = lax.dot_general(eye, k[:, j * d:(j + 1) * d], nt, preferred_element_type=F32)
                kt_buf[j, c] = kt.astype(BF16)
            return carry

        lax.fori_loop(0, nk, transpose, 0)

    slope = jnp.float32(slopes[0])
    for j in range(1, HEADS_B):
        slope = jnp.where(h == j, jnp.float32(slopes[j]), slope)
    q = q_ref[0]
    qs = [(q[:, j * d:(j + 1) * d].astype(F32) * (d ** -0.5)).astype(BF16) for j in range(2)]
    colf = lax.broadcasted_iota(jnp.int32, (1, tq), 1).astype(F32)
    nl = tq // LANES

    def lane_fold(x, op, acc):
        for c in range(nl):
            acc = op(acc, x[:, c * LANES:(c + 1) * LANES])
        return acc

    m_buf[...] = jnp.full(m_buf.shape, NEG, F32)

    def scores(kb, masked):
        bias = slope * (colf + ((kb - qi) * tq).astype(F32))
        for j in range(2):
            s = jnp.dot(qs[j], kt_buf[j, kb], preferred_element_type=F32) + bias
            if masked:
                row = lax.broadcasted_iota(jnp.int32, (tq, tq), 0)
                col = lax.broadcasted_iota(jnp.int32, (tq, tq), 1)
                s = jnp.where(col <= row, s, NEG)
            s_buf[j, kb] = s
            m_buf[j] = lane_fold(s, jnp.maximum, m_buf[j])

    def scores_body(kb, carry):
        scores(kb, False)
        return carry

    lax.fori_loop(0, qi, scores_body, 0)
    scores(qi, True)
    ms = [jnp.max(m_buf[j], axis=-1, keepdims=True) for j in range(2)]

    acc_buf[...] = jnp.zeros_like(acc_buf)
    ones_col = (lax.broadcasted_iota(jnp.int32, (tq, 2 * d), 1) == 0).astype(BF16)

    def weights(kb, carry):
        v = v_ref[0, pl.ds(pl.multiple_of(kb * tq, tq), tq), :]
        v1 = jnp.concatenate([v, ones_col], axis=1)
        for j in range(2):
            p = jnp.exp(s_buf[j, kb] - ms[j])
            acc_buf[j] = acc_buf[j] + jnp.dot(p.astype(BF16), v1, preferred_element_type=F32)
        return carry

    lax.fori_loop(0, qi + 1, weights, 0)
    a0, a1 = acc_buf[0, :, 0:2 * d], acc_buf[1, :, 0:2 * d]
    l0, l1 = acc_buf[0, :, 2 * d:2 * d + 1], acc_buf[1, :, 2 * d:2 * d + 1]
    lv = lv_ref[...]
    lam = (jnp.exp(jnp.sum(lv[0:1] * lv[1:2], keepdims=True)) - jnp.exp(jnp.sum(lv[2:3] * lv[3:4], keepdims=True))
           + lambda_init)
    o = a0 / l0 - lam * (a1 / l1)
    o = o * lax.rsqrt(jnp.mean(o * o, axis=-1, keepdims=True) + LN_EPS)
    o = o * g_ref[...] * (1.0 - lambda_init)
    o_ref[0] = o.astype(o_ref.dtype)


def _mixer_b(proj_bc, lam_vecs, subln_g, lambda_init, tq=512):
    b, s, _ = proj_bc.shape
    w = 2 * HEAD_DIM_B
    nk = s // tq
    kern = functools.partial(_attn_b_kernel, slopes=_alibi_slopes(HEADS_B), lambda_init=lambda_init, tq=tq)
    return pl.pallas_call(
        kern,
        grid=(b, HEADS_B, nk),
        in_specs=[pl.BlockSpec((1, tq, w), lambda bi, h, i: (bi, i, h)),
                  pl.BlockSpec((1, s, w), lambda bi, h, i: (bi, 0, HEADS_B + h)),
                  pl.BlockSpec((1, s, w), lambda bi, h, i: (bi, 0, 2 * HEADS_B + h)),
                  pl.BlockSpec(lam_vecs.shape, lambda bi, h, i: (0, 0)),
                  pl.BlockSpec((1, w), lambda bi, h, i: (0, 0))],
        out_specs=pl.BlockSpec((1, tq, w), lambda bi, h, i: (bi, i, h)),
        out_shape=jax.ShapeDtypeStruct((b, s, HEADS_B * w), BF16),
        scratch_shapes=[pltpu.VMEM((2, nk, HEAD_DIM_B, tq), BF16), pltpu.VMEM((2, nk, tq, tq), F32),
                        pltpu.VMEM((2, tq, LANES), F32), pltpu.VMEM((2, tq, 2 * w), F32)],
        compiler_params=_params("arbitrary", "arbitrary", "arbitrary"),
        name="attn_b",
    )(proj_bc, proj_bc, proj_bc, lam_vecs, subln_g.reshape(1, w))


def _attn_c_kernel(sink_ref, q_ref, kp_ref, kc_ref, vp_ref, vc_ref, o_ref, kbuf, vbuf, *, slopes, nb):
    n = pl.program_id(1)
    d = HEAD_DIM_C
    half = lax.broadcasted_iota(jnp.int32, (1, 2 * d), 1) < d

    for j in range(KV_HEADS_C):
        for src_p, src_c, dst in ((kp_ref, kc_ref, kbuf), (vp_ref, vc_ref, vbuf)):
            xp = src_p[0, :, j * d:(j + 1) * d]
            xc = src_c[0, :, j * d:(j + 1) * d]
            dst[j, 0:BLOCK, :] = jnp.concatenate([xp, xp], axis=1)
            dst[j, BLOCK:, :] = jnp.concatenate([xc, xc], axis=1)

    row = lax.broadcasted_iota(jnp.int32, (BLOCK, 2 * BLOCK), 0)
    col = lax.broadcasted_iota(jnp.int32, (BLOCK, 2 * BLOCK), 1)
    dist = row + BLOCK - col
    band = (dist >= 0) & (dist <= WINDOW_C - 1)
    first_lim = jnp.where(n > 0, 0, BLOCK)
    band_first = band & (col >= first_lim)
    distf = dist.astype(F32)
    scale = d ** -0.5
    zero = jnp.zeros((), BF16)
    for hp in range(HEADS_C // 2):
        kv = (2 * hp) // GQA_GROUP_C
        for i in range(nb // BLOCK):
            rows = slice(i * BLOCK, (i + 1) * BLOCK)
            qp = q_ref[0, rows, hp * 2 * d:(hp + 1) * 2 * d]
            k = kbuf[kv, i * BLOCK:(i + 2) * BLOCK, :]
            v = vbuf[kv, i * BLOCK:(i + 2) * BLOCK, :]
            outs = []
            for e in range(2):
                h = 2 * hp + e
                q = jnp.where(half if e == 0 else ~half, qp, zero)
                s = lax.dot_general(q, k, (((1,), (1,)), ((), ())), preferred_element_type=F32) * scale - slopes[h] * distf
                s = jnp.where(band_first if i == 0 else band, s, NEG)
                sink = sink_ref[h]
                m = jnp.maximum(jnp.max(s, axis=-1, keepdims=True), sink)
                p = jnp.exp(s - m)
                denom = jnp.sum(p, axis=-1, keepdims=True) + jnp.exp(sink - m)
                outs.append(jnp.dot(p.astype(BF16), v, preferred_element_type=F32) / denom)
            o_ref[0, rows, hp * 2 * d:(hp + 1) * 2 * d] = jnp.where(half, outs[0], outs[1]).astype(o_ref.dtype)


def _mixer_c(proj_bc, sinks):
    b, s, _ = proj_bc.shape
    nb = min(512, s)
    sub = nb // BLOCK
    wq = HEADS_C * HEAD_DIM_C
    wkv = KV_HEADS_C * HEAD_DIM_C
    q0 = 3 * HEADS_B * 2 * HEAD_DIM_B
    cq, ck, cv = q0 // wq, (q0 + wq) // wkv, (q0 + wq + wkv) // wkv
    kern = functools.partial(_attn_c_kernel, slopes=_alibi_slopes(HEADS_C), nb=nb)
    prev = lambda n: jnp.maximum(n * sub - 1, 0)
    return pl.pallas_call(
        kern,
        grid=(b, s // nb),
        in_specs=[pl.BlockSpec(memory_space=pltpu.SMEM),
                  pl.BlockSpec((1, nb, wq), lambda bi, n: (bi, n, cq)),
                  pl.BlockSpec((1, BLOCK, wkv), lambda bi, n: (bi, prev(n), ck)),
                  pl.BlockSpec((1, nb, wkv), lambda bi, n: (bi, n, ck)),
                  pl.BlockSpec((1, BLOCK, wkv), lambda bi, n: (bi, prev(n), cv)),
                  pl.BlockSpec((1, nb, wkv), lambda bi, n: (bi, n, cv))],
        out_specs=pl.BlockSpec((1, nb, wq), lambda bi, n: (bi, n, 0)),
        out_shape=jax.ShapeDtypeStruct((b, s, wq), BF16),
        scratch_shapes=[pltpu.VMEM((KV_HEADS_C, BLOCK + nb, 2 * HEAD_DIM_C), BF16),
                        pltpu.VMEM((KV_HEADS_C, BLOCK + nb, 2 * HEAD_DIM_C), BF16)],
        compiler_params=_params("arbitrary", "arbitrary"),
        name="attn_c",
    )(sinks, proj_bc, proj_bc, proj_bc, proj_bc, proj_bc)


def _merge_kernel(o1_ref, o2_ref, o3_ref, l1_ref, l2_ref, l3_ref, yb_ref, yc_ref, g_ref, wb_ref, out_ref, obuf):
    la, lb, lc = l1_ref[...], l2_ref[...], l3_ref[...]
    m = jnp.maximum(jnp.maximum(la, lb), lc)
    ea, eb, ec = jnp.exp(la - m), jnp.exp(lb - m), jnp.exp(lc - m)
    den = ea + eb + ec
    wts = (ea / den, eb / den, ec / den)
    tm = out_ref.shape[0]
    for gi, (o_ref, (_, r)) in enumerate(zip((o1_ref, o2_ref, o3_ref), DILATED_GROUPS)):
        for rho in range(r):
            for h in range(HEADS_A):
                hs = slice(h * HEAD_DIM_A, (h + 1) * HEAD_DIM_A)
                if r == 1:
                    obuf[gi, h] = o_ref[0, 0, :, hs].astype(F32)
                else:
                    obuf[gi, h, pl.ds(rho, tm // r, stride=r), :] = o_ref[0, rho, :, hs].astype(F32)
    parts = []
    for h in range(HEADS_A):
        y = wts[0][:, h:h + 1] * obuf[0, h] + wts[1][:, h:h + 1] * obuf[1, h] + wts[2][:, h:h + 1] * obuf[2, h]
        parts.append(y.astype(BF16))
    ya = jnp.concatenate(parts, axis=1)
    dm = out_ref.shape[1]
    acc = g_ref[:, 0:dm].astype(F32) * jnp.dot(ya, wb_ref[0, 0], preferred_element_type=F32)
    acc = acc + g_ref[:, dm:2 * dm].astype(F32) * jnp.dot(yb_ref[...], wb_ref[0, 1], preferred_element_type=F32)
    acc = acc + g_ref[:, 2 * dm:3 * dm].astype(F32) * jnp.dot(yc_ref[...], wb_ref[0, 2], preferred_element_type=F32)
    out_ref[...] = acc.astype(out_ref.dtype)


def _merge(outs_a, lses_a, y_b, y_c, gates, w_branch, layer, seq, tm=512):
    t = y_b.shape[0]
    dm = w_branch.shape[3]
    per_b = seq // tm
    wide = lambda w: pl.BlockSpec((tm, w), lambda i: (i, 0))
    group = lambda r: pl.BlockSpec((1, r, tm // r, GROUP_WIDTH_A), lambda i: (i // per_b, 0, i % per_b, 0))
    return pl.pallas_call(
        _merge_kernel,
        grid=(t // tm,),
        in_specs=[group(r) for _, r in DILATED_GROUPS] + [wide(HEADS_A)] * 3 + [wide(BRANCH_WIDTH)] * 2
                 + [wide(3 * dm)] + [pl.BlockSpec((1,) + w_branch.shape[1:], lambda i: (layer, 0, 0, 0))],
        out_specs=wide(dm),
        out_shape=jax.ShapeDtypeStruct((t, dm), BF16),
        scratch_shapes=[pltpu.VMEM((N_GROUPS_A, HEADS_A, tm, HEAD_DIM_A), F32)],
        compiler_params=_params("arbitrary"),
        name="merge",
    )(*outs_a, *lses_a, y_b, y_c, gates, w_branch)


def _layer_norm(z, g, b):
    mu = jnp.mean(z, axis=-1, keepdims=True)
    zc = z - mu
    var = jnp.mean(zc * zc, axis=-1, keepdims=True)
    return zc * lax.rsqrt(var + LN_EPS) * g + b


def _wo_ln_route_kernel(a_ref, wo_ref, x_ref, g1_ref, sh_ref, sc_ref, lng_ref, lnb_ref, whi_ref, wlo_ref, br_ref,
                        xo_ref, route_ref, cnt_ref, run_ref, mix_buf, hhi_buf, hlo_buf, before_buf, *, alpha, tm):
    @pl.when(pl.program_id(0) == 0)
    def _():
        run_ref[...] = jnp.zeros_like(run_ref)
        r_i = lax.broadcasted_iota(jnp.int32, (tm, tm), 0)
        c_i = lax.broadcasted_iota(jnp.int32, (tm, tm), 1)
        before_buf[...] = jnp.where(c_i < r_i, 1.0, 0.0).astype(BF16)

    mix_buf[...] = jnp.dot(a_ref[...], wo_ref[0], preferred_element_type=F32)

    def rows(r, carry):
        rs = pl.ds(pl.multiple_of(r * ROW_CHUNK, ROW_CHUNK), ROW_CHUNK)
        xn = _layer_norm(alpha * x_ref[rs, :] + g1_ref[0] * mix_buf[rs, :], lng_ref[...], lnb_ref[...])
        xo_ref[rs, :] = xn
        h = xn * (1.0 + sc_ref[0]) + sh_ref[0]
        hi = h.astype(BF16)
        hhi_buf[rs, :] = hi
        hlo_buf[rs, :] = (h - hi.astype(F32)).astype(BF16)
        return carry

    lax.fori_loop(0, tm // ROW_CHUNK, rows, 0, unroll=ROW_UNROLL)

    hhi = hhi_buf[...]
    logits = (jnp.dot(hhi, whi_ref[...], preferred_element_type=F32)
              + jnp.dot(hlo_buf[...], whi_ref[...], preferred_element_type=F32)
              + jnp.dot(hhi, wlo_ref[...], preferred_element_type=F32) + br_ref[...])
    lane = lax.broadcasted_iota(jnp.int32, (tm, LANES), 1).astype(F32)
    far = float(4 * LANES)
    gl = jnp.where(lane < N_EXPERT_GROUPS, logits, NEG)
    gm = jnp.max(gl, axis=-1, keepdims=True)
    g_val = 1.0 / jnp.sum(jnp.exp(gl - gm), axis=-1, keepdims=True)
    g_idx = jnp.min(jnp.where(gl == gm, lane, far), axis=-1, keepdims=True)
    lo = ROUTE_LANE0 + EXPERTS_PER_GROUP * g_idx
    fl = jnp.where((lane >= lo) & (lane < lo + EXPERTS_PER_GROUP), logits, NEG)
    m1 = jnp.max(fl, axis=-1, keepdims=True)
    i1 = jnp.min(jnp.where(fl == m1, lane, far), axis=-1, keepdims=True)
    fl2 = jnp.where(lane == i1, NEG, fl)
    m2 = jnp.max(fl2, axis=-1, keepdims=True)
    i2 = jnp.min(jnp.where(fl2 == m2, lane, far), axis=-1, keepdims=True)
    e = jnp.exp(m2 - m1)
    w1 = g_val / (1.0 + e)
    w2 = g_val * e / (1.0 + e)

    oh1 = lane == i1
    oh2 = lane == i2
    c = jnp.where(oh1 | oh2, 1.0, 0.0)
    tot = jnp.dot(before_buf[...], c.astype(BF16), preferred_element_type=F32) + run_ref[...]
    rank1 = jnp.sum(jnp.where(oh1, tot, 0.0), axis=-1, keepdims=True)
    rank2 = jnp.sum(jnp.where(oh2, tot, 0.0), axis=-1, keepdims=True)
    run_ref[...] = run_ref[...] + jnp.sum(c, axis=0, keepdims=True)
    cnt_ref[...] = run_ref[...]

    l8 = lax.broadcasted_iota(jnp.int32, (tm, 8), 1)
    e1 = i1 - ROUTE_LANE0
    e2 = i2 - ROUTE_LANE0
    route = jnp.where(l8 == 0, w1, jnp.where(l8 == 1, w2, jnp.where(l8 == 2, e1, jnp.where(l8 == 3, e2,
            jnp.where(l8 == 4, rank1, rank2)))))
    route_ref[...] = route


def _wo_ln_route(merged, w_o, layer, x2, ada_l, ln_g, ln_b, w_r_hi, w_r_lo, b_r, seq, alpha, tm=512):
    t, d = x2.shape
    per_b = seq // tm
    vec = lambda k: pl.BlockSpec((1, 1, d), lambda i: (i // per_b, 0, k))
    row = lambda w: pl.BlockSpec((tm, w), lambda i: (i, 0))
    full = lambda a: pl.BlockSpec(a.shape, lambda i: (0,) * a.ndim)
    kern = functools.partial(_wo_ln_route_kernel, alpha=alpha, tm=tm)
    return pl.pallas_call(
        kern,
        grid=(t // tm,),
        in_specs=[row(d), pl.BlockSpec((1, d, d), lambda i: (layer, 0, 0)), row(d), vec(2), vec(3), vec(4),
                  full(ln_g), full(ln_b), full(w_r_hi), full(w_r_lo), full(b_r)],
        out_specs=[row(d), row(8), pl.BlockSpec((1, LANES), lambda i: (0, 0))],
        out_shape=[jax.ShapeDtypeStruct((t, d), F32), jax.ShapeDtypeStruct((t, 8), F32),
                   jax.ShapeDtypeStruct((1, LANES), F32)],
        scratch_shapes=[pltpu.VMEM((1, LANES), F32), pltpu.VMEM((tm, d), F32), pltpu.VMEM((tm, d), BF16),
                        pltpu.VMEM((tm, d), BF16), pltpu.VMEM((tm, tm), BF16)],
        compiler_params=_params("arbitrary"),
        name="wo_ln_route",
    )(merged, w_o, x2, ada_l, ada_l, ada_l, ln_g, ln_b, w_r_hi, w_r_lo, b_r)


def _row_copy(src, i, dst, j, sem):
    return pltpu.make_async_copy(src.at[pl.ds(i, 1), :], dst.at[pl.ds(j, 1), :], sem)


def _dispatch_kernel(p0_ref, p1_ref, x_ref, sh_ref, sc_ref, xs_hbm, hbuf, sem, *, tc):
    step = pl.program_id(0)
    slot = step % 2
    stage = hbuf.at[slot]

    def stage_and_issue(k, c):
        base = pl.multiple_of(k * ROW_CHUNK, ROW_CHUNK)
        rs = pl.ds(base, ROW_CHUNK)
        stage[rs, :] = x_ref[rs, :] * (1.0 + sc_ref[0]) + sh_ref[0]
        for r in range(ROW_CHUNK):
            _row_copy(stage, base + r, xs_hbm, p0_ref[base + r], sem.at[slot]).start()
            _row_copy(stage, base + r, xs_hbm, p1_ref[base + r], sem.at[slot]).start()
        return c

    lax.fori_loop(0, tc // ROW_CHUNK, stage_and_issue, 0)

    def drain(which):
        def body(i, c):
            _row_copy(hbuf.at[which], 0, xs_hbm, 0, sem.at[which]).wait()
            return c

        lax.fori_loop(0, 2 * tc, body, 0, unroll=8)

    @pl.when(step > 0)
    def _():
        drain(1 - slot)

    @pl.when(step == pl.num_programs(0) - 1)
    def _():
        drain(slot)


def _dispatch(x2, ada_l, pos0, pos1, seq, tc=256):
    t, d = x2.shape
    per_b = seq // tc
    vec = lambda k: pl.BlockSpec((1, 1, d), lambda i: (i // per_b, 0, k))
    return pl.pallas_call(
        functools.partial(_dispatch_kernel, tc=tc),
        grid=(t // tc,),
        in_specs=[pl.BlockSpec((tc,), lambda i: (i,), memory_space=pltpu.SMEM),
                  pl.BlockSpec((tc,), lambda i: (i,), memory_space=pltpu.SMEM),
                  pl.BlockSpec((tc, d), lambda i: (i, 0)), vec(3), vec(4)],
        out_specs=pl.BlockSpec(memory_space=pl.ANY),
        out_shape=jax.ShapeDtypeStruct((2 * t, d), F32),
        scratch_shapes=[pltpu.VMEM((2, tc, d), F32), pltpu.SemaphoreType.DMA((2,))],
        compiler_params=_params("arbitrary"),
        name="dispatch",
    )(pos0, pos1, x2, ada_l, ada_l)


def _expert_kernel(it_ref, ie_ref, lo_ref, hi_ref, first_ref, nu_ref, x_ref, w1_ref, w3_ref, w2_ref, y_ref,
                   w1b, w3b, w2b, *, tile):
    i = pl.program_id(0)

    @pl.when((i == 0) | (ie_ref[i] != ie_ref[jnp.maximum(i - 1, 0)]))
    def _():
        w1b[...] = w1_ref[0, 0].astype(BF16)
        w3b[...] = w3_ref[0, 0].astype(BF16)
        w2b[...] = w2_ref[0, 0].astype(BF16)

    @pl.when(i < nu_ref[0])
    def _():
        x = x_ref[...].astype(BF16)
        a = jnp.dot(x, w1b[...], preferred_element_type=F32)
        b = jnp.dot(x, w3b[...], preferred_element_type=F32)
        he = (a * jax.nn.sigmoid(a) * b).astype(BF16)
        y = jnp.dot(he, w2b[...], preferred_element_type=F32)
        row = it_ref[i] * tile + lax.broadcasted_iota(jnp.int32, (tile, 1), 0)
        mine = (row >= lo_ref[i]) & (row < hi_ref[i])

        @pl.when(first_ref[i] == 1)
        def _():
            y_ref[...] = jnp.where(mine, y, 0.0)

        @pl.when(first_ref[i] == 0)
        def _():
            y_ref[...] = jnp.where(mine, y, y_ref[...])


def _experts(xs, w1, w3, w2, layer, plan):
    n_rows, d = xs.shape
    de = w1.shape[3]
    tile = EXPERT_TILE
    n_items = plan[0].shape[0]
    row_map = lambda i, it, ie, lo, hi, fi, nu: (it[i], 0)
    w_map = lambda i, it, ie, lo, hi, fi, nu: (layer, ie[i], 0, 0)
    grid_spec = pltpu.PrefetchScalarGridSpec(
        num_scalar_prefetch=6,
        grid=(n_items,),
        in_specs=[pl.BlockSpec((tile, d), row_map),
                  pl.BlockSpec((1, 1, d, de), w_map),
                  pl.BlockSpec((1, 1, d, de), w_map),
                  pl.BlockSpec((1, 1, de, d), w_map)],
        out_specs=pl.BlockSpec((tile, d), row_map),
        scratch_shapes=[pltpu.VMEM((d, de), BF16), pltpu.VMEM((d, de), BF16), pltpu.VMEM((de, d), BF16)],
    )
    return pl.pallas_call(
        functools.partial(_expert_kernel, tile=tile),
        grid_spec=grid_spec,
        out_shape=jax.ShapeDtypeStruct((n_rows, d), F32),
        compiler_params=pltpu.CompilerParams(dimension_semantics=("arbitrary",), vmem_limit_bytes=EXPERT_VMEM_LIMIT),
        name="experts",
    )(*plan, xs, w1, w3, w2)


def _combine_kernel(p0_ref, p1_ref, p0n_ref, p1n_ref, ys_hbm, route_ref, x_ref, g2_ref, lng_ref, lnb_ref, sh_ref, sc_ref,
                    xo_ref, h_ref, buf, sem, *, alpha, tc):
    step = pl.program_id(0)
    slot = step % 2

    def issue_rows(pa_ref, pb_ref, s, base, n):
        for r in range(n):
            _row_copy(ys_hbm, pa_ref[base + r], buf.at[s, 0], base + r, sem.at[s]).start()
            _row_copy(ys_hbm, pb_ref[base + r], buf.at[s, 1], base + r, sem.at[s]).start()

    @pl.when(step == 0)
    def _():
        def issue(k, c):
            issue_rows(p0_ref, p1_ref, 0, pl.multiple_of(k * SUBLANES, SUBLANES), SUBLANES)
            return c

        lax.fori_loop(0, tc // SUBLANES, issue, 0)

    def drain(i, c):
        _row_copy(ys_hbm, 0, buf.at[slot, 0], 0, sem.at[slot]).wait()
        return c

    lax.fori_loop(0, 2 * tc, drain, 0, unroll=8)

    def ln_rows(r):
        rs = pl.ds(pl.multiple_of(r * ROW_CHUNK, ROW_CHUNK), ROW_CHUNK)
        w = route_ref[rs, :]
        ffn = w[:, 0:1] * buf[slot, 0, rs, :] + w[:, 1:2] * buf[slot, 1, rs, :]
        xn = _layer_norm(alpha * x_ref[rs, :] + g2_ref[0] * ffn, lng_ref[...], lnb_ref[...])
        xo_ref[rs, :] = xn
        h_ref[rs, :] = (xn * (1.0 + sc_ref[0]) + sh_ref[0]).astype(h_ref.dtype)

    group = ROW_CHUNK * ROW_UNROLL

    @pl.when(step + 1 < pl.num_programs(0))
    def _():
        def body(g, carry):
            issue_rows(p0n_ref, p1n_ref, 1 - slot, pl.multiple_of(g * group, group), group)
            for u in range(ROW_UNROLL):
                ln_rows(g * ROW_UNROLL + u)
            return carry

        lax.fori_loop(0, tc // group, body, 0)

    @pl.when(step + 1 == pl.num_programs(0))
    def _():
        def body(g, carry):
            for u in range(ROW_UNROLL):
                ln_rows(g * ROW_UNROLL + u)
            return carry

        lax.fori_loop(0, tc // group, body, 0)


def _combine(ys, pos0, pos1, route, x2, ada_l, ada_next, ln_g, ln_b, seq, alpha, tc=512):
    t, d = x2.shape
    per_b = seq // tc
    n = t // tc
    vec = lambda k: pl.BlockSpec((1, 1, d), lambda i: (i // per_b, 0, k))
    row = lambda w: pl.BlockSpec((tc, w), lambda i: (i, 0))
    full = lambda a: pl.BlockSpec(a.shape, lambda i: (0,) * a.ndim)
    kern = functools.partial(_combine_kernel, alpha=alpha, tc=tc)
    return pl.pallas_call(
        kern,
        grid=(n,),
        in_specs=[pl.BlockSpec((tc,), lambda i: (i,), memory_space=pltpu.SMEM),
                  pl.BlockSpec((tc,), lambda i: (i,), memory_space=pltpu.SMEM),
                  pl.BlockSpec((tc,), lambda i: (jnp.minimum(i + 1, n - 1),), memory_space=pltpu.SMEM),
                  pl.BlockSpec((tc,), lambda i: (jnp.minimum(i + 1, n - 1),), memory_space=pltpu.SMEM),
                  pl.BlockSpec(memory_space=pl.ANY),
                  row(8), row(d), vec(5), full(ln_g), full(ln_b), vec(0), vec(1)],
        out_specs=[row(d), row(d)],
        out_shape=[jax.ShapeDtypeStruct((t, d), F32), jax.ShapeDtypeStruct((t, d), BF16)],
        scratch_shapes=[pltpu.VMEM((2, 2, tc, d), F32), pltpu.SemaphoreType.DMA((2,))],
        compiler_params=_params("arbitrary"),
        name="combine",
    )(pos0, pos1, pos0, pos1, ys, route, x2, ada_l, ln_g, ln_b, ada_next, ada_next)


def _positions_kernel(route_ref, starts_ref, pos_ref):
    r = route_ref[...]
    tm = r.shape[0]
    li = lax.broadcasted_iota(jnp.int32, (tm, LANES), 1)
    lane = li.astype(F32)
    starts = starts_ref[...]

    def start_of(e):
        return jnp.sum(jnp.where(lane == e + ROUTE_LANE0, starts, 0.0), axis=-1, keepdims=True)

    p1 = start_of(r[:, 2:3]) + r[:, 4:5]
    p2 = start_of(r[:, 3:4]) + r[:, 5:6]
    cols = jnp.where(li == 0, p1, jnp.where(li == 1, p2, 0.0))
    pos_ref[...] = cols.T[0:8, :].astype(jnp.int32)


def _positions(route, starts, tm=2048):
    t = route.shape[0]
    tm = min(tm, t)
    return pl.pallas_call(
        _positions_kernel,
        grid=(t // tm,),
        in_specs=[pl.BlockSpec((tm, 8), lambda i: (i, 0)), pl.BlockSpec((1, LANES), lambda i: (0, 0))],
        out_specs=pl.BlockSpec((8, tm), lambda i: (0, i)),
        out_shape=jax.ShapeDtypeStruct((8, t), jnp.int32),
        compiler_params=_params("arbitrary"),
        name="positions",
    )(route, starts)


def _dispatch_plan(counts, n_rows):
    tile = EXPERT_TILE
    n_items = n_rows // tile + N_EXPERTS
    starts_lanes = jnp.cumsum(counts, axis=1) - counts
    cnt = counts[0, ROUTE_LANE0:ROUTE_LANE0 + N_EXPERTS].astype(jnp.int32)
    ends = jnp.cumsum(cnt)
    starts = ends - cnt
    first_tile = starts // tile
    n_e = jnp.where(cnt > 0, (ends - 1) // tile - first_tile + 1, 0)
    item_ends = jnp.cumsum(n_e)
    n_used = item_ends[-1]
    k = jnp.minimum(jnp.arange(n_items, dtype=jnp.int32), n_used - 1)
    item_expert = jnp.minimum(jnp.sum((item_ends[None, :] <= k[:, None]).astype(jnp.int32), axis=1), N_EXPERTS - 1)
    onehot = (item_expert[:, None] == jnp.arange(N_EXPERTS, dtype=jnp.int32)[None, :]).astype(jnp.int32)
    pick = lambda v: jnp.sum(onehot * v[None, :], axis=1)
    item_tile = pick(first_tile) + k - pick(item_ends - n_e)
    item_first = jnp.concatenate([jnp.ones((1,), jnp.int32), (item_tile[1:] != item_tile[:-1]).astype(jnp.int32)])
    plan = (item_tile, item_expert, pick(starts), pick(ends), item_first, n_used.reshape(1))
    return starts_lanes, tuple(p.astype(jnp.int32) for p in plan)


def kernel(x, c, w_ada, b_ada, w_in, w_gate, b_gate, w_branch, w_o, lam_vecs, subln_g, sinks, ln_g, ln_b,
           w_rg, b_rg, w_rf, b_rf, w1, w3, w2):
    b, s, d = x.shape
    depth = w_ada.shape[0]
    t = b * s
    alpha = (2 * depth) ** 0.25
    assert s % (16 * BLOCK) == 0 and d % LANES == 0

    w_in16 = w_in.astype(BF16)
    w_gate16 = w_gate.astype(BF16)
    w_branch16 = w_branch.astype(BF16)
    w_o16 = w_o.astype(BF16)
    pad = LANES - N_EXPERT_GROUPS - N_EXPERTS
    w_r = jnp.concatenate([w_rg, w_rf, jnp.zeros((depth, d, pad), F32)], axis=-1)
    w_r_hi = w_r.astype(BF16)
    w_r_lo = (w_r - w_r_hi.astype(F32)).astype(BF16)
    b_r = jnp.concatenate([b_rg, b_rf, jnp.zeros((depth, pad), F32)], axis=-1).reshape(depth, 1, LANES)

    b_gate4 = b_gate.reshape(depth, -1, 1, d)
    ada = _ada(c, w_ada, b_ada).reshape(depth, b, 1, 6 * d)
    x2 = x.reshape(t, d)
    h = _modulate(x2, ada[0], s)

    for l in range(depth):
        lambda_init = 0.8 - 0.6 * float(np.exp(-0.3 * l))
        ada_l = ada[l]
        n_bc = w_in.shape[2] - 3 * QKV_A
        qkv_groups = _proj_a(h, w_in16, l, b, s)
        proj_bc = _matmul(h, w_in16, l, 3 * QKV_A, n_bc).reshape(b, s, n_bc)
        gates = _gates(h, w_gate16, b_gate4, l)
        outs_a, lses_a = _mixer_a(qkv_groups)
        y_b = _mixer_b(proj_bc, lam_vecs[l], subln_g[l], lambda_init)
        y_c = _mixer_c(proj_bc, sinks[l])
        merged = _merge(outs_a, lses_a, y_b.reshape(t, -1), y_c.reshape(t, -1), gates, w_branch16, l, s)
        x2, route, counts = _wo_ln_route(merged, w_o16, l, x2, ada_l, ln_g[l, 0:1], ln_b[l, 0:1],
                                         w_r_hi[l], w_r_lo[l], b_r[l], s, alpha)
        starts_lanes, plan = _dispatch_plan(counts, 2 * t)
        pos = _positions(route, starts_lanes)
        pos0, pos1 = pos[0], pos[1]
        xs = _dispatch(x2, ada_l, pos0, pos1, s)
        ys = _experts(xs, w1, w3, w2, l, plan)
        ada_next = ada[min(l + 1, depth - 1)]
        x2, h = _combine(ys, pos0, pos1, route, x2, ada_l, ada_next, ln_g[l, 1:2], ln_b[l, 1:2], s, alpha)
    return x2.reshape(b, s, d)
```

```python
import functools

import numpy as np
import jax
import jax.numpy as jnp
from jax import lax
from jax.experimental import pallas as pl
from jax.experimental.pallas import tpu as pltpu

F32 = jnp.float32
BF16 = jnp.bfloat16
HIGHEST = lax.Precision.HIGHEST

BLOCK = 128
DILATED_GROUPS = ((128, 1), (512, 4), (2048, 16))
N_GROUPS_A = 3
HEADS_A = 4
HEAD_DIM_A = 128
GROUP_WIDTH_A = HEADS_A * HEAD_DIM_A
QKV_A = N_GROUPS_A * GROUP_WIDTH_A
HEADS_B = 4
HEAD_DIM_B = 64
HEADS_C = 8
KV_HEADS_C = 2
GQA_GROUP_C = HEADS_C // KV_HEADS_C
HEAD_DIM_C = 64
WINDOW_C = 128
BRANCH_WIDTH = 512
N_EXPERT_GROUPS = 4
EXPERTS_PER_GROUP = 8
N_EXPERTS = N_EXPERT_GROUPS * EXPERTS_PER_GROUP
LN_EPS = 1e-5
NEG = -1e30

LANES = 128
SUBLANES = 8
ROUTE_LANE0 = N_EXPERT_GROUPS
VMEM_LIMIT = 56 * 1024 * 1024
EXPERT_VMEM_LIMIT = 60 * 1024 * 1024

EXPERT_TILE = 512
ROW_CHUNK = 16
ROW_UNROLL = 4


def _alibi_slopes(n):
    return [float(v) for v in 2.0 ** (-8.0 * np.arange(1, n + 1, dtype=np.float32) / n)]


def _params(*sem):
    return pltpu.CompilerParams(dimension_semantics=sem, vmem_limit_bytes=VMEM_LIMIT)


def _ada_kernel(c_ref, w_ref, b_ref, o_ref):
    c = c_ref[...]
    cond = c * jax.nn.sigmoid(c)
    o_ref[0] = jnp.dot(cond, w_ref[0], preferred_element_type=F32, precision=HIGHEST) + b_ref[0]


def _ada(c, w_ada, b_ada):
    depth, d, n = w_ada.shape
    bsz = c.shape[0]
    tn = 1024
    return pl.pallas_call(
        _ada_kernel,
        grid=(depth, n // tn),
        in_specs=[pl.BlockSpec((bsz, d), lambda l, j: (0, 0)),
                  pl.BlockSpec((1, d, tn), lambda l, j: (l, 0, j)),
                  pl.BlockSpec((1, 1, tn), lambda l, j: (l, 0, j))],
        out_specs=pl.BlockSpec((1, bsz, tn), lambda l, j: (l, 0, j)),
        out_shape=jax.ShapeDtypeStruct((depth, bsz, n), F32),
        compiler_params=_params("arbitrary", "arbitrary"),
        name="ada",
    )(c, w_ada, b_ada.reshape(depth, 1, n))


def _modulate_kernel(x_ref, sh_ref, sc_ref, o_ref):
    o_ref[...] = (x_ref[...] * (1.0 + sc_ref[0]) + sh_ref[0]).astype(o_ref.dtype)


def _modulate(x2, ada_l, seq, tm=512):
    t, d = x2.shape
    per_b = seq // tm
    return pl.pallas_call(
        _modulate_kernel,
        grid=(t // tm,),
        in_specs=[pl.BlockSpec((tm, d), lambda i: (i, 0)),
                  pl.BlockSpec((1, 1, d), lambda i: (i // per_b, 0, 0)),
                  pl.BlockSpec((1, 1, d), lambda i: (i // per_b, 0, 1))],
        out_specs=pl.BlockSpec((tm, d), lambda i: (i, 0)),
        out_shape=jax.ShapeDtypeStruct((t, d), BF16),
        compiler_params=_params("arbitrary"),
        name="modulate",
    )(x2, ada_l, ada_l)


def _mm_kernel(x_ref, w_ref, o_ref):
    o_ref[...] = jnp.dot(x_ref[...], w_ref[0], preferred_element_type=F32).astype(o_ref.dtype)


def _matmul(x2, w, layer, col0, n, tm=2048, tn=768):
    t, d = x2.shape
    tm = min(tm, t)
    c0 = col0 // tn
    assert col0 % tn == 0 and n % tn == 0
    return pl.pallas_call(
        _mm_kernel,
        grid=(t // tm, n // tn),
        in_specs=[pl.BlockSpec((tm, d), lambda i, j: (i, 0)),
                  pl.BlockSpec((1, d, tn), lambda i, j: (layer, 0, c0 + j))],
        out_specs=pl.BlockSpec((tm, tn), lambda i, j: (i, j)),
        out_shape=jax.ShapeDtypeStruct((t, n), BF16),
        compiler_params=_params("arbitrary", "arbitrary"),
        name="proj",
    )(x2, w)


def _proj_a_kernel(x_ref, w_ref, o0_ref, o1_ref, o2_ref, buf):
    g = pl.program_id(1) % N_GROUPS_A
    acc = jnp.dot(x_ref[...], w_ref[0], preferred_element_type=F32)
    tm = acc.shape[0]
    for gi, (o_ref, (_, r)) in enumerate(zip((o0_ref, o1_ref, o2_ref), DILATED_GROUPS)):
        @pl.when(g == gi)
        def _(o_ref=o_ref, r=r):
            if r == 1:
                o_ref[0, 0] = acc.astype(o_ref.dtype)
            else:
                for c in range(GROUP_WIDTH_A // LANES):
                    buf[c] = acc[:, c * LANES:(c + 1) * LANES]
                for rho in range(r):
                    for c in range(GROUP_WIDTH_A // LANES):
                        o_ref[0, rho, :, c * LANES:(c + 1) * LANES] = (
                            buf[c, pl.ds(rho, tm // r, stride=r), :].astype(o_ref.dtype))


def _proj_a(x2, w, layer, bsz, seq, tm=2048):
    t, d = x2.shape
    tm = min(tm, seq)
    per_b = seq // tm
    w_g = GROUP_WIDTH_A
    n_tiles = 3 * N_GROUPS_A

    def out_spec(g, r):
        return pl.BlockSpec((1, r, tm // r, w_g),
                            lambda i, j: (i // per_b, 0, i % per_b, jnp.maximum(j - g, 0) // N_GROUPS_A))

    return pl.pallas_call(
        _proj_a_kernel,
        grid=(t // tm, n_tiles),
        in_specs=[pl.BlockSpec((tm, d), lambda i, j: (i, 0)),
                  pl.BlockSpec((1, d, w_g), lambda i, j: (layer, 0, j))],
        out_specs=[out_spec(g, r) for g, (_, r) in enumerate(DILATED_GROUPS)],
        out_shape=[jax.ShapeDtypeStruct((bsz, r, seq // r, 3 * w_g), BF16) for _, r in DILATED_GROUPS],
        scratch_shapes=[pltpu.VMEM((w_g // LANES, tm, LANES), F32)],
        compiler_params=_params("arbitrary", "arbitrary"),
        name="proj_a",
    )(x2, w)


def _gate_kernel(x_ref, w_ref, b_ref, o_ref):
    acc = jnp.dot(x_ref[...], w_ref[0, 0], preferred_element_type=F32) + b_ref[0, 0]
    o_ref[...] = jax.nn.sigmoid(acc).astype(o_ref.dtype)


def _gates(x2, w_gate, b_gate, layer, tm=2048, tn=1024):
    t, d = x2.shape
    _, nbr, _, n = w_gate.shape
    tm = min(tm, t)
    per = n // tn
    return pl.pallas_call(
        _gate_kernel,
        grid=(t // tm, nbr * per),
        in_specs=[pl.BlockSpec((tm, d), lambda i, j: (i, 0)),
                  pl.BlockSpec((1, 1, d, tn), lambda i, j: (layer, j // per, 0, j % per)),
                  pl.BlockSpec((1, 1, 1, tn), lambda i, j: (layer, j // per, 0, j % per))],
        out_specs=pl.BlockSpec((tm, tn), lambda i, j: (i, j)),
        out_shape=jax.ShapeDtypeStruct((t, nbr * n), BF16),
        compiler_params=_params("arbitrary", "arbitrary"),
        name="gates",
    )(x2, w_gate, b_gate)


def _attn_a_kernel(q_ref, kp_ref, kc_ref, vp_ref, vc_ref, o_ref, lse_ref, kbuf, vbuf, *, slopes, dist_unit, nb):
    n = pl.program_id(1)
    kbuf[0:BLOCK, :] = kp_ref[0]
    kbuf[BLOCK:, :] = kc_ref[0]
    vbuf[0:BLOCK, :] = vp_ref[0]
    vbuf[BLOCK:, :] = vc_ref[0]
    row = lax.broadcasted_iota(jnp.int32, (BLOCK, 2 * BLOCK), 0)
    col = lax.broadcasted_iota(jnp.int32, (BLOCK, 2 * BLOCK), 1)
    dist = row + BLOCK - col
    band = (dist >= 0) & (dist <= BLOCK)
    first_lim = jnp.where(n > 0, 0, BLOCK)
    band_first = band & (col >= first_lim)
    distf = dist.astype(F32) * float(dist_unit)
    scale = HEAD_DIM_A ** -0.5
    for h in range(HEADS_A):
        hs = slice(h * HEAD_DIM_A, (h + 1) * HEAD_DIM_A)
        bias = -slopes[h] * distf
        for i in range(nb // BLOCK):
            rows = slice(i * BLOCK, (i + 1) * BLOCK)
            q = q_ref[0, rows, hs]
            k = kbuf[i * BLOCK:(i + 2) * BLOCK, hs]
            v = vbuf[i * BLOCK:(i + 2) * BLOCK, hs]
            s = lax.dot_general(q, k, (((1,), (1,)), ((), ())), preferred_element_type=F32) * scale + bias
            s = jnp.where(band_first if i == 0 else band, s, NEG)
            m = jnp.max(s, axis=-1, keepdims=True)
            p = jnp.exp(s - m)
            l = jnp.sum(p, axis=-1, keepdims=True)
            o = jnp.dot(p.astype(BF16), v, preferred_element_type=F32) / l
            o_ref[0, rows, hs] = o.astype(o_ref.dtype)
            lse_ref[0, rows, h:h + 1] = m + jnp.log(l)


def _attn_a_group(qkv, cols, slopes, dist_unit):
    nseq, l, _ = qkv.shape
    nb = min(1024, l)
    sub = nb // BLOCK
    cq, ck, cv = cols
    w = GROUP_WIDTH_A
    kern = functools.partial(_attn_a_kernel, slopes=slopes, dist_unit=dist_unit, nb=nb)
    prev = lambda b, n: jnp.maximum(n * sub - 1, 0)
    return pl.pallas_call(
        kern,
        grid=(nseq, l // nb),
        in_specs=[pl.BlockSpec((1, nb, w), lambda b, n: (b, n, cq)),
                  pl.BlockSpec((1, BLOCK, w), lambda b, n: (b, prev(b, n), ck)),
                  pl.BlockSpec((1, nb, w), lambda b, n: (b, n, ck)),
                  pl.BlockSpec((1, BLOCK, w), lambda b, n: (b, prev(b, n), cv)),
                  pl.BlockSpec((1, nb, w), lambda b, n: (b, n, cv))],
        out_specs=[pl.BlockSpec((1, nb, w), lambda b, n: (b, n, 0)),
                   pl.BlockSpec((1, nb, HEADS_A), lambda b, n: (b, n, 0))],
        out_shape=[jax.ShapeDtypeStruct((nseq, l, w), BF16),
                   jax.ShapeDtypeStruct((nseq, l, HEADS_A), F32)],
        scratch_shapes=[pltpu.VMEM((BLOCK + nb, w), BF16), pltpu.VMEM((BLOCK + nb, w), BF16)],
        compiler_params=_params("arbitrary", "arbitrary"),
        name="attn_a",
    )(qkv, qkv, qkv, qkv, qkv)


def _mixer_a(qkv_groups):
    slopes = _alibi_slopes(N_GROUPS_A * HEADS_A)
    outs, lses = [], []
    for g, (window, r) in enumerate(DILATED_GROUPS):
        assert window // r == BLOCK
        b, _, l, w = qkv_groups[g].shape
        o, lse = _attn_a_group(qkv_groups[g].reshape(b * r, l, w), (0, 1, 2), slopes[g * HEADS_A:(g + 1) * HEADS_A], r)
        outs.append(o.reshape(b, r, l, GROUP_WIDTH_A))
        lses.append(lse.reshape(b, r, l, HEADS_A).transpose(0, 2, 1, 3).reshape(b * l * r, HEADS_A))
    return outs, lses


def _attn_b_kernel(q_ref, k_ref, v_ref, lv_ref, g_ref, o_ref, kt_buf, s_buf, m_buf, acc_buf, *,
                   slopes, lambda_init, tq):
    h = pl.program_id(1)
    qi = pl.program_id(2)
    d = HEAD_DIM_B
    nk = kt_buf.shape[1]
    nt = (((1,), (1,)), ((), ()))

    @pl.when(qi == 0)
    def _():
        eye = (lax.broadcasted_iota(jnp.int32, (d, d), 0) == lax.broadcasted_iota(jnp.int32, (d, d), 1)).astype(BF16)

        def transpose(c, carry):
            k = k_ref[0, pl.ds(pl.multiple_of(c * tq, tq), tq), :]
            for j in range(2):
                kt = lax.dot_general(eye, k[:, j * d:(j + 1) * d], nt, preferred_element_type=F32)
                kt_buf[j, c] = kt.astype(BF16)
            return carry

        lax.fori_loop(0, nk, transpose, 0)

    slope = jnp.float32(slopes[0])
    for j in range(1, HEADS_B):
        slope = jnp.where(h == j, jnp.float32(slopes[j]), slope)
    q = q_ref[0]
    qs = [(q[:, j * d:(j + 1) * d].astype(F32) * (d ** -0.5)).astype(BF16) for j in range(2)]
    colf = lax.broadcasted_iota(jnp.int32, (1, tq), 1).astype(F32)
    nl = tq // LANES

    def lane_fold(x, op, acc):
        for c in range(nl):
            acc = op(acc, x[:, c * LANES:(c + 1) * LANES])
        return acc

    m_buf[...] = jnp.full(m_buf.shape, NEG, F32)

    def scores(kb, masked):
        bias = slope * (colf + ((kb - qi) * tq).astype(F32))
        for j in range(2):
            s = jnp.dot(qs[j], kt_buf[j, kb], preferred_element_type=F32) + bias
            if masked:
                row = lax.broadcasted_iota(jnp.int32, (tq, tq), 0)
                col = lax.broadcasted_iota(jnp.int32, (tq, tq), 1)
                s = jnp.where(col <= row, s, NEG)
            s_buf[j, kb] = s
            m_buf[j] = lane_fold(s, jnp.maximum, m_buf[j])

    def scores_body(kb, carry):
        scores(kb, False)
        return carry

    lax.fori_loop(0, qi, scores_body, 0)
    scores(qi, True)
    ms = [jnp.max(m_buf[j], axis=-1, keepdims=True) for j in range(2)]

    acc_buf[...] = jnp.zeros_like(acc_buf)
    ones_col = (lax.broadcasted_iota(jnp.int32, (tq, 2 * d), 1) == 0).astype(BF16)

    def weights(kb, carry):
        v = v_ref[0, pl.ds(pl.multiple_of(kb * tq, tq), tq), :]
        v1 = jnp.concatenate([v, ones_col], axis=1)
        for j in range(2):
            p = jnp.exp(s_buf[j, kb] - ms[j])
            acc_buf[j] = acc_buf[j] + jnp.dot(p.astype(BF16), v1, preferred_element_type=F32)
        return carry

    lax.fori_loop(0, qi + 1, weights, 0)
    a0, a1 = acc_buf[0, :, 0:2 * d], acc_buf[1, :, 0:2 * d]
    l0, l1 = acc_buf[0, :, 2 * d:2 * d + 1], acc_buf[1, :, 2 * d:2 * d + 1]
    lv = lv_ref[...]
    lam = (jnp.exp(jnp.sum(lv[0:1] * lv[1:2], keepdims=True)) - jnp.exp(jnp.sum(lv[2:3] * lv[3:4], keepdims=True))
           + lambda_init)
    o = a0 / l0 - lam * (a1 / l1)
    o = o * lax.rsqrt(jnp.mean(o * o, axis=-1, keepdims=True) + LN_EPS)
    o = o * g_ref[...] * (1.0 - lambda_init)
    o_ref[0] = o.astype(o_ref.dtype)


def _mixer_b(proj_bc, lam_vecs, subln_g, lambda_init, tq=512):
    b, s, _ = proj_bc.shape
    w = 2 * HEAD_DIM_B
    nk = s // tq
    kern = functools.partial(_attn_b_kernel, slopes=_alibi_slopes(HEADS_B), lambda_init=lambda_init, tq=tq)
    return pl.pallas_call(
        kern,
        grid=(b, HEADS_B, nk),
        in_specs=[pl.BlockSpec((1, tq, w), lambda bi, h, i: (bi, i, h)),
                  pl.BlockSpec((1, s, w), lambda bi, h, i: (bi, 0, HEADS_B + h)),
                  pl.BlockSpec((1, s, w), lambda bi, h, i: (bi, 0, 2 * HEADS_B + h)),
                  pl.BlockSpec(lam_vecs.shape, lambda bi, h, i: (0, 0)),
                  pl.BlockSpec((1, w), lambda bi, h, i: (0, 0))],
        out_specs=pl.BlockSpec((1, tq, w), lambda bi, h, i: (bi, i, h)),
        out_shape=jax.ShapeDtypeStruct((b, s, HEADS_B * w), BF16),
        scratch_shapes=[pltpu.VMEM((2, nk, HEAD_DIM_B, tq), BF16), pltpu.VMEM((2, nk, tq, tq), F32),
                        pltpu.VMEM((2, tq, LANES), F32), pltpu.VMEM((2, tq, 2 * w), F32)],
        compiler_params=_params("arbitrary", "arbitrary", "arbitrary"),
        name="attn_b",
    )(proj_bc, proj_bc, proj_bc, lam_vecs, subln_g.reshape(1, w))


def _attn_c_kernel(sink_ref, q_ref, kp_ref, kc_ref, vp_ref, vc_ref, o_ref, kbuf, vbuf, *, slopes, nb):
    n = pl.program_id(1)
    d = HEAD_DIM_C
    half = lax.broadcasted_iota(jnp.int32, (1, 2 * d), 1) < d

    for j in range(KV_HEADS_C):
        for src_p, src_c, dst in ((kp_ref, kc_ref, kbuf), (vp_ref, vc_ref, vbuf)):
            xp = src_p[0, :, j * d:(j + 1) * d]
            xc = src_c[0, :, j * d:(j + 1) * d]
            dst[j, 0:BLOCK, :] = jnp.concatenate([xp, xp], axis=1)
            dst[j, BLOCK:, :] = jnp.concatenate([xc, xc], axis=1)

    row = lax.broadcasted_iota(jnp.int32, (BLOCK, 2 * BLOCK), 0)
    col = lax.broadcasted_iota(jnp.int32, (BLOCK, 2 * BLOCK), 1)
    dist = row + BLOCK - col
    band = (dist >= 0) & (dist <= WINDOW_C - 1)
    first_lim = jnp.where(n > 0, 0, BLOCK)
    band_first = band & (col >= first_lim)
    distf = dist.astype(F32)
    scale = d ** -0.5
    zero = jnp.zeros((), BF16)
    for hp in range(HEADS_C // 2):
        kv = (2 * hp) // GQA_GROUP_C
        for i in range(nb // BLOCK):
            rows = slice(i * BLOCK, (i + 1) * BLOCK)
            qp = q_ref[0, rows, hp * 2 * d:(hp + 1) * 2 * d]
            k = kbuf[kv, i * BLOCK:(i + 2) * BLOCK, :]
            v = vbuf[kv, i * BLOCK:(i + 2) * BLOCK, :]
            outs = []
            for e in range(2):
                h = 2 * hp + e
                q = jnp.where(half if e == 0 else ~half, qp, zero)
                s = lax.dot_general(q, k, (((1,), (1,)), ((), ())), preferred_element_type=F32) * scale - slopes[h] * distf
                s = jnp.where(band_first if i == 0 else band, s, NEG)
                sink = sink_ref[h]
                m = jnp.maximum(jnp.max(s, axis=-1, keepdims=True), sink)
                p = jnp.exp(s - m)
                denom = jnp.sum(p, axis=-1, keepdims=True) + jnp.exp(sink - m)
                outs.append(jnp.dot(p.astype(BF16), v, preferred_element_type=F32) / denom)
            o_ref[0, rows, hp * 2 * d:(hp + 1) * 2 * d] = jnp.where(half, outs[0], outs[1]).astype(o_ref.dtype)


def _mixer_c(proj_bc, sinks):
    b, s, _ = proj_bc.shape
    nb = min(512, s)
    sub = nb // BLOCK
    wq = HEADS_C * HEAD_DIM_C
    wkv = KV_HEADS_C * HEAD_DIM_C
    q0 = 3 * HEADS_B * 2 * HEAD_DIM_B
    cq, ck, cv = q0 // wq, (q0 + wq) // wkv, (q0 + wq + wkv) // wkv
    kern = functools.partial(_attn_c_kernel, slopes=_alibi_slopes(HEADS_C), nb=nb)
    prev = lambda n: jnp.maximum(n * sub - 1, 0)
    return pl.pallas_call(
        kern,
        grid=(b, s // nb),
        in_specs=[pl.BlockSpec(memory_space=pltpu.SMEM),
                  pl.BlockSpec((1, nb, wq), lambda bi, n: (bi, n, cq)),
                  pl.BlockSpec((1, BLOCK, wkv), lambda bi, n: (bi, prev(n), ck)),
                  pl.BlockSpec((1, nb, wkv), lambda bi, n: (bi, n, ck)),
                  pl.BlockSpec((1, BLOCK, wkv), lambda bi, n: (bi, prev(n), cv)),
                  pl.BlockSpec((1, nb, wkv), lambda bi, n: (bi, n, cv))],
        out_specs=pl.BlockSpec((1, nb, wq), lambda bi, n: (bi, n, 0)),
        out_shape=jax.ShapeDtypeStruct((b, s, wq), BF16),
        scratch_shapes=[pltpu.VMEM((KV_HEADS_C, BLOCK + nb, 2 * HEAD_DIM_C), BF16),
                        pltpu.VMEM((KV_HEADS_C, BLOCK + nb, 2 * HEAD_DIM_C), BF16)],
        compiler_params=_params("arbitrary", "arbitrary"),
        name="attn_c",
    )(sinks, proj_bc, proj_bc, proj_bc, proj_bc, proj_bc)


def _merge_kernel(o1_ref, o2_ref, o3_ref, l1_ref, l2_ref, l3_ref, yb_ref, yc_ref, g_ref, wb_ref, out_ref, obuf):
    la, lb, lc = l1_ref[...], l2_ref[...], l3_ref[...]
    m = jnp.maximum(jnp.maximum(la, lb), lc)
    ea, eb, ec = jnp.exp(la - m), jnp.exp(lb - m), jnp.exp(lc - m)
    den = ea + eb + ec
    wts = (ea / den, eb / den, ec / den)
    tm = out_ref.shape[0]
    for gi, (o_ref, (_, r)) in enumerate(zip((o1_ref, o2_ref, o3_ref), DILATED_GROUPS)):
        for rho in range(r):
            for h in range(HEADS_A):
                hs = slice(h * HEAD_DIM_A, (h + 1) * HEAD_DIM_A)
                if r == 1:
                    obuf[gi, h] = o_ref[0, 0, :, hs].astype(F32)
                else:
                    obuf[gi, h, pl.ds(rho, tm // r, stride=r), :] = o_ref[0, rho, :, hs].astype(F32)
    parts = []
    for h in range(HEADS_A):
        y = wts[0][:, h:h + 1] * obuf[0, h] + wts[1][:, h:h + 1] * obuf[1, h] + wts[2][:, h:h + 1] * obuf[2, h]
        parts.append(y.astype(BF16))
    ya = jnp.concatenate(parts, axis=1)
    dm = out_ref.shape[1]
    acc = g_ref[:, 0:dm].astype(F32) * jnp.dot(ya, wb_ref[0, 0], preferred_element_type=F32)
    acc = acc + g_ref[:, dm:2 * dm].astype(F32) * jnp.dot(yb_ref[...], wb_ref[0, 1], preferred_element_type=F32)
    acc = acc + g_ref[:, 2 * dm:3 * dm].astype(F32) * jnp.dot(yc_ref[...], wb_ref[0, 2], preferred_element_type=F32)
    out_ref[...] = acc.astype(out_ref.dtype)


def _merge(outs_a, lses_a, y_b, y_c, gates, w_branch, layer, seq, tm=512):
    t = y_b.shape[0]
    dm = w_branch.shape[3]
    per_b = seq // tm
    wide = lambda w: pl.BlockSpec((tm, w), lambda i: (i, 0))
    group = lambda r: pl.BlockSpec((1, r, tm // r, GROUP_WIDTH_A), lambda i: (i // per_b, 0, i % per_b, 0))
    return pl.pallas_call(
        _merge_kernel,
        grid=(t // tm,),
        in_specs=[group(r) for _, r in DILATED_GROUPS] + [wide(HEADS_A)] * 3 + [wide(BRANCH_WIDTH)] * 2
                 + [wide(3 * dm)] + [pl.BlockSpec((1,) + w_branch.shape[1:], lambda i: (layer, 0, 0, 0))],
        out_specs=wide(dm),
        out_shape=jax.ShapeDtypeStruct((t, dm), BF16),
        scratch_shapes=[pltpu.VMEM((N_GROUPS_A, HEADS_A, tm, HEAD_DIM_A), F32)],
        compiler_params=_params("arbitrary"),
        name="merge",
    )(*outs_a, *lses_a, y_b, y_c, gates, w_branch)


def _layer_norm(z, g, b):
    mu = jnp.mean(z, axis=-1, keepdims=True)
    zc = z - mu
    var = jnp.mean(zc * zc, axis=-1, keepdims=True)
    return zc * lax.rsqrt(var + LN_EPS) * g + b


def _wo_ln_route_kernel(a_ref, wo_ref, x_ref, g1_ref, sh_ref, sc_ref, lng_ref, lnb_ref, whi_ref, wlo_ref, br_ref,
                        xo_ref, route_ref, cnt_ref, run_ref, mix_buf, hhi_buf, hlo_buf, before_buf, *, alpha, tm):
    @pl.when(pl.program_id(0) == 0)
    def _():
        run_ref[...] = jnp.zeros_like(run_ref)
        r_i = lax.broadcasted_iota(jnp.int32, (tm, tm), 0)
        c_i = lax.broadcasted_iota(jnp.int32, (tm, tm), 1)
        before_buf[...] = jnp.where(c_i < r_i, 1.0, 0.0).astype(BF16)

    mix_buf[...] = jnp.dot(a_ref[...], wo_ref[0], preferred_element_type=F32)

    def rows(r, carry):
        rs = pl.ds(pl.multiple_of(r * ROW_CHUNK, ROW_CHUNK), ROW_CHUNK)
        xn = _layer_norm(alpha * x_ref[rs, :] + g1_ref[0] * mix_buf[rs, :], lng_ref[...], lnb_ref[...])
        xo_ref[rs, :] = xn
        h = xn * (1.0 + sc_ref[0]) + sh_ref[0]
        hi = h.astype(BF16)
        hhi_buf[rs, :] = hi
        hlo_buf[rs, :] = (h - hi.astype(F32)).astype(BF16)
        return carry

    lax.fori_loop(0, tm // ROW_CHUNK, rows, 0, unroll=ROW_UNROLL)

    hhi = hhi_buf[...]
    logits = (jnp.dot(hhi, whi_ref[...], preferred_element_type=F32)
              + jnp.dot(hlo_buf[...], whi_ref[...], preferred_element_type=F32)
              + jnp.dot(hhi, wlo_ref[...], preferred_element_type=F32) + br_ref[...])
    lane = lax.broadcasted_iota(jnp.int32, (tm, LANES), 1).astype(F32)
    far = float(4 * LANES)
    gl = jnp.where(lane < N_EXPERT_GROUPS, logits, NEG)
    gm = jnp.max(gl, axis=-1, keepdims=True)
    g_val = 1.0 / jnp.sum(jnp.exp(gl - gm), axis=-1, keepdims=True)
    g_idx = jnp.min(jnp.where(gl == gm, lane, far), axis=-1, keepdims=True)
    lo = ROUTE_LANE0 + EXPERTS_PER_GROUP * g_idx
    fl = jnp.where((lane >= lo) & (lane < lo + EXPERTS_PER_GROUP), logits, NEG)
    m1 = jnp.max(fl, axis=-1, keepdims=True)
    i1 = jnp.min(jnp.where(fl == m1, lane, far), axis=-1, keepdims=True)
    fl2 = jnp.where(lane == i1, NEG, fl)
    m2 = jnp.max(fl2, axis=-1, keepdims=True)
    i2 = jnp.min(jnp.where(fl2 == m2, lane, far), axis=-1, keepdims=True)
    e = jnp.exp(m2 - m1)
    w1 = g_val / (1.0 + e)
    w2 = g_val * e / (1.0 + e)

    oh1 = lane == i1
    oh2 = lane == i2
    c = jnp.where(oh1 | oh2, 1.0, 0.0)
    tot = jnp.dot(before_buf[...], c.astype(BF16), preferred_element_type=F32) + run_ref[...]
    rank1 = jnp.sum(jnp.where(oh1, tot, 0.0), axis=-1, keepdims=True)
    rank2 = jnp.sum(jnp.where(oh2, tot, 0.0), axis=-1, keepdims=True)
    run_ref[...] = run_ref[...] + jnp.sum(c, axis=0, keepdims=True)
    cnt_ref[...] = run_ref[...]

    l8 = lax.broadcasted_iota(jnp.int32, (tm, 8), 1)
    e1 = i1 - ROUTE_LANE0
    e2 = i2 - ROUTE_LANE0
    route = jnp.where(l8 == 0, w1, jnp.where(l8 == 1, w2, jnp.where(l8 == 2, e1, jnp.where(l8 == 3, e2,
            jnp.where(l8 == 4, rank1, rank2)))))
    route_ref[...] = route


def _wo_ln_route(merged, w_o, layer, x2, ada_l, ln_g, ln_b, w_r_hi, w_r_lo, b_r, seq, alpha, tm=512):
    t, d = x2.shape
    per_b = seq // tm
    vec = lambda k: pl.BlockSpec((1, 1, d), lambda i: (i // per_b, 0, k))
    row = lambda w: pl.BlockSpec((tm, w), lambda i: (i, 0))
    full = lambda a: pl.BlockSpec(a.shape, lambda i: (0,) * a.ndim)
    kern = functools.partial(_wo_ln_route_kernel, alpha=alpha, tm=tm)
    return pl.pallas_call(
        kern,
        grid=(t // tm,),
        in_specs=[row(d), pl.BlockSpec((1, d, d), lambda i: (layer, 0, 0)), row(d), vec(2), vec(3), vec(4),
                  full(ln_g), full(ln_b), full(w_r_hi), full(w_r_lo), full(b_r)],
        out_specs=[row(d), row(8), pl.BlockSpec((1, LANES), lambda i: (0, 0))],
        out_shape=[jax.ShapeDtypeStruct((t, d), F32), jax.ShapeDtypeStruct((t, 8), F32),
                   jax.ShapeDtypeStruct((1, LANES), F32)],
        scratch_shapes=[pltpu.VMEM((1, LANES), F32), pltpu.VMEM((tm, d), F32), pltpu.VMEM((tm, d), BF16),
                        pltpu.VMEM((tm, d), BF16), pltpu.VMEM((tm, tm), BF16)],
        compiler_params=_params("arbitrary"),
        name="wo_ln_route",
    )(merged, w_o, x2, ada_l, ada_l, ada_l, ln_g, ln_b, w_r_hi, w_r_lo, b_r)


def _row_copy(src, i, dst, j, sem):
    return pltpu.make_async_copy(src.at[pl.ds(i, 1), :], dst.at[pl.ds(j, 1), :], sem)


def _dispatch_kernel(p0_ref, p1_ref, x_ref, sh_ref, sc_ref, xs_hbm, hbuf, sem, *, tc):
    step = pl.program_id(0)
    slot = step % 2
    stage = hbuf.at[slot]

    def stage_and_issue(k, c):
        base = pl.multiple_of(k * ROW_CHUNK, ROW_CHUNK)
        rs = pl.ds(base, ROW_CHUNK)
        stage[rs, :] = x_ref[rs, :] * (1.0 + sc_ref[0]) + sh_ref[0]
        for r in range(ROW_CHUNK):
            _row_copy(stage, base + r, xs_hbm, p0_ref[base + r], sem.at[slot]).start()
            _row_copy(stage, base + r, xs_hbm, p1_ref[base + r], sem.at[slot]).start()
        return c

    lax.fori_loop(0, tc // ROW_CHUNK, stage_and_issue, 0)

    def drain(which):
        def body(i, c):
            _row_copy(hbuf.at[which], 0, xs_hbm, 0, sem.at[which]).wait()
            return c

        lax.fori_loop(0, 2 * tc, body, 0, unroll=8)

    @pl.when(step > 0)
    def _():
        drain(1 - slot)

    @pl.when(step == pl.num_programs(0) - 1)
    def _():
        drain(slot)


def _dispatch(x2, ada_l, pos0, pos1, seq, tc=256):
    t, d = x2.shape
    per_b = seq // tc
    vec = lambda k: pl.BlockSpec((1, 1, d), lambda i: (i // per_b, 0, k))
    return pl.pallas_call(
        functools.partial(_dispatch_kernel, tc=tc),
        grid=(t // tc,),
        in_specs=[pl.BlockSpec((tc,), lambda i: (i,), memory_space=pltpu.SMEM),
                  pl.BlockSpec((tc,), lambda i: (i,), memory_space=pltpu.SMEM),
                  pl.BlockSpec((tc, d), lambda i: (i, 0)), vec(3), vec(4)],
        out_specs=pl.BlockSpec(memory_space=pl.ANY),
        out_shape=jax.ShapeDtypeStruct((2 * t, d), F32),
        scratch_shapes=[pltpu.VMEM((2, tc, d), F32), pltpu.SemaphoreType.DMA((2,))],
        compiler_params=_params("arbitrary"),
        name="dispatch",
    )(pos0, pos1, x2, ada_l, ada_l)


def _expert_kernel(it_ref, ie_ref, lo_ref, hi_ref, first_ref, nu_ref, x_ref, w1_ref, w3_ref, w2_ref, y_ref,
                   w1b, w3b, w2b, *, tile):
    i = pl.program_id(0)

    @pl.when((i == 0) | (ie_ref[i] != ie_ref[jnp.maximum(i - 1, 0)]))
    def _():
        w1b[...] = w1_ref[0, 0].astype(BF16)
        w3b[...] = w3_ref[0, 0].astype(BF16)
        w2b[...] = w2_ref[0, 0].astype(BF16)

    @pl.when(i < nu_ref[0])
    def _():
        x = x_ref[...].astype(BF16)
        a = jnp.dot(x, w1b[...], preferred_element_type=F32)
        b = jnp.dot(x, w3b[...], preferred_element_type=F32)
        he = (a * jax.nn.sigmoid(a) * b).astype(BF16)
        y = jnp.dot(he, w2b[...], preferred_element_type=F32)
        start = it_ref[i] * tile
        row = start + lax.broadcasted_iota(jnp.int32, (tile, 1), 0)
        mine = (row >= lo_ref[i]) & (row < hi_ref[i])
        whole = (lo_ref[i] <= start) & (hi_ref[i] >= start + tile)

        @pl.when(whole)
        def _():
            y_ref[...] = y

        @pl.when(jnp.logical_not(whole) & (first_ref[i] == 1))
        def _():
            y_ref[...] = jnp.where(mine, y, 0.0)

        @pl.when(jnp.logical_not(whole) & (first_ref[i] == 0))
        def _():
            y_ref[...] = jnp.where(mine, y, y_ref[...])


def _experts(xs, w1, w3, w2, layer, plan):
    n_rows, d = xs.shape
    de = w1.shape[3]
    tile = EXPERT_TILE
    n_items = plan[0].shape[0]
    row_map = lambda i, it, ie, lo, hi, fi, nu: (it[i], 0)
    w_map = lambda i, it, ie, lo, hi, fi, nu: (layer, ie[i], 0, 0)
    grid_spec = pltpu.PrefetchScalarGridSpec(
        num_scalar_prefetch=6,
        grid=(n_items,),
        in_specs=[pl.BlockSpec((tile, d), row_map),
                  pl.BlockSpec((1, 1, d, de), w_map),
                  pl.BlockSpec((1, 1, d, de), w_map),
                  pl.BlockSpec((1, 1, de, d), w_map)],
        out_specs=pl.BlockSpec((tile, d), row_map),
        scratch_shapes=[pltpu.VMEM((d, de), BF16), pltpu.VMEM((d, de), BF16), pltpu.VMEM((de, d), BF16)],
    )
    return pl.pallas_call(
        functools.partial(_expert_kernel, tile=tile),
        grid_spec=grid_spec,
        out_shape=jax.ShapeDtypeStruct((n_rows, d), F32),
        compiler_params=pltpu.CompilerParams(dimension_semantics=("arbitrary",), vmem_limit_bytes=EXPERT_VMEM_LIMIT),
        name="experts",
    )(*plan, xs, w1, w3, w2)


def _combine_kernel(p0_ref, p1_ref, p0n_ref, p1n_ref, ys_hbm, route_ref, x_ref, g2_ref, lng_ref, lnb_ref, sh_ref, sc_ref,
                    xo_ref, h_ref, buf, sem, *, alpha, tc):
    step = pl.program_id(0)
    slot = step % 2

    def issue_rows(pa_ref, pb_ref, s, base, n):
        for r in range(n):
            _row_copy(ys_hbm, pa_ref[base + r], buf.at[s, 0], base + r, sem.at[s]).start()
            _row_copy(ys_hbm, pb_ref[base + r], buf.at[s, 1], base + r, sem.at[s]).start()

    @pl.when(step == 0)
    def _():
        def issue(k, c):
            issue_rows(p0_ref, p1_ref, 0, pl.multiple_of(k * SUBLANES, SUBLANES), SUBLANES)
            return c

        lax.fori_loop(0, tc // SUBLANES, issue, 0)

    def drain(i, c):
        _row_copy(ys_hbm, 0, buf.at[slot, 0], 0, sem.at[slot]).wait()
        return c

    lax.fori_loop(0, 2 * tc, drain, 0, unroll=8)

    def ln_rows(r):
        rs = pl.ds(pl.multiple_of(r * ROW_CHUNK, ROW_CHUNK), ROW_CHUNK)
        w = route_ref[rs, :]
        ffn = w[:, 0:1] * buf[slot, 0, rs, :] + w[:, 1:2] * buf[slot, 1, rs, :]
        xn = _layer_norm(alpha * x_ref[rs, :] + g2_ref[0] * ffn, lng_ref[...], lnb_ref[...])
        xo_ref[rs, :] = xn
        h_ref[rs, :] = (xn * (1.0 + sc_ref[0]) + sh_ref[0]).astype(h_ref.dtype)

    group = ROW_CHUNK * ROW_UNROLL

    @pl.when(step + 1 < pl.num_programs(0))
    def _():
        def body(g, carry):
            issue_rows(p0n_ref, p1n_ref, 1 - slot, pl.multiple_of(g * group, group), group)
            for u in range(ROW_UNROLL):
                ln_rows(g * ROW_UNROLL + u)
            return carry

        lax.fori_loop(0, tc // group, body, 0)

    @pl.when(step + 1 == pl.num_programs(0))
    def _():
        def body(g, carry):
            for u in range(ROW_UNROLL):
                ln_rows(g * ROW_UNROLL + u)
            return carry

        lax.fori_loop(0, tc // group, body, 0)


def _combine(ys, pos0, pos1, route, x2, ada_l, ada_next, ln_g, ln_b, seq, alpha, tc=512):
    t, d = x2.shape
    per_b = seq // tc
    n = t // tc
    vec = lambda k: pl.BlockSpec((1, 1, d), lambda i: (i // per_b, 0, k))
    row = lambda w: pl.BlockSpec((tc, w), lambda i: (i, 0))
    full = lambda a: pl.BlockSpec(a.shape, lambda i: (0,) * a.ndim)
    kern = functools.partial(_combine_kernel, alpha=alpha, tc=tc)
    return pl.pallas_call(
        kern,
        grid=(n,),
        in_specs=[pl.BlockSpec((tc,), lambda i: (i,), memory_space=pltpu.SMEM),
                  pl.BlockSpec((tc,), lambda i: (i,), memory_space=pltpu.SMEM),
                  pl.BlockSpec((tc,), lambda i: (jnp.minimum(i + 1, n - 1),), memory_space=pltpu.SMEM),
                  pl.BlockSpec((tc,), lambda i: (jnp.minimum(i + 1, n - 1),), memory_space=pltpu.SMEM),
                  pl.BlockSpec(memory_space=pl.ANY),
                  row(8), row(d), vec(5), full(ln_g), full(ln_b), vec(0), vec(1)],
        out_specs=[row(d), row(d)],
        out_shape=[jax.ShapeDtypeStruct((t, d), F32), jax.ShapeDtypeStruct((t, d), BF16)],
        scratch_shapes=[pltpu.VMEM((2, 2, tc, d), F32), pltpu.SemaphoreType.DMA((2,))],
        compiler_params=_params("arbitrary"),
        name="combine",
    )(pos0, pos1, pos0, pos1, ys, route, x2, ada_l, ln_g, ln_b, ada_next, ada_next)


def _positions_kernel(route_ref, starts_ref, pos_ref):
    r = route_ref[...]
    tm = r.shape[0]
    li = lax.broadcasted_iota(jnp.int32, (tm, LANES), 1)
    lane = li.astype(F32)
    starts = starts_ref[...]

    def start_of(e):
        return jnp.sum(jnp.where(lane == e + ROUTE_LANE0, starts, 0.0), axis=-1, keepdims=True)

    p1 = start_of(r[:, 2:3]) + r[:, 4:5]
    p2 = start_of(r[:, 3:4]) + r[:, 5:6]
    cols = jnp.where(li == 0, p1, jnp.where(li == 1, p2, 0.0))
    pos_ref[...] = cols.T[0:8, :].astype(jnp.int32)


def _positions(route, starts, tm=2048):
    t = route.shape[0]
    tm = min(tm, t)
    return pl.pallas_call(
        _positions_kernel,
        grid=(t // tm,),
        in_specs=[pl.BlockSpec((tm, 8), lambda i: (i, 0)), pl.BlockSpec((1, LANES), lambda i: (0, 0))],
        out_specs=pl.BlockSpec((8, tm), lambda i: (0, i)),
        out_shape=jax.ShapeDtypeStruct((8, t), jnp.int32),
        compiler_params=_params("arbitrary"),
        name="positions",
    )(route, starts)


def _dispatch_plan(counts, n_rows):
    tile = EXPERT_TILE
    n_items = n_rows // tile + N_EXPERTS
    starts_lanes = jnp.cumsum(counts, axis=1) - counts
    cnt = counts[0, ROUTE_LANE0:ROUTE_LANE0 + N_EXPERTS].astype(jnp.int32)
    ends = jnp.cumsum(cnt)
    starts = ends - cnt
    first_tile = starts // tile
    n_e = jnp.where(cnt > 0, (ends - 1) // tile - first_tile + 1, 0)
    item_ends = jnp.cumsum(n_e)
    n_used = item_ends[-1]
    k = jnp.minimum(jnp.arange(n_items, dtype=jnp.int32), n_used - 1)
    item_expert = jnp.minimum(jnp.sum((item_ends[None, :] <= k[:, None]).astype(jnp.int32), axis=1), N_EXPERTS - 1)
    onehot = (item_expert[:, None] == jnp.arange(N_EXPERTS, dtype=jnp.int32)[None, :]).astype(jnp.int32)
    pick = lambda v: jnp.sum(onehot * v[None, :], axis=1)
    item_tile = pick(first_tile) + k - pick(item_ends - n_e)
    item_first = jnp.concatenate([jnp.ones((1,), jnp.int32), (item_tile[1:] != item_tile[:-1]).astype(jnp.int32)])
    plan = (item_tile, item_expert, pick(starts), pick(ends), item_first, n_used.reshape(1))
    return starts_lanes, tuple(p.astype(jnp.int32) for p in plan)


def kernel(x, c, w_ada, b_ada, w_in, w_gate, b_gate, w_branch, w_o, lam_vecs, subln_g, sinks, ln_g, ln_b,
           w_rg, b_rg, w_rf, b_rf, w1, w3, w2):
    b, s, d = x.shape
    depth = w_ada.shape[0]
    t = b * s
    alpha = (2 * depth) ** 0.25
    assert s % (16 * BLOCK) == 0 and d % LANES == 0

    w_in16 = w_in.astype(BF16)
    w_gate16 = w_gate.astype(BF16)
    w_branch16 = w_branch.astype(BF16)
    w_o16 = w_o.astype(BF16)
    pad = LANES - N_EXPERT_GROUPS - N_EXPERTS
    w_r = jnp.concatenate([w_rg, w_rf, jnp.zeros((depth, d, pad), F32)], axis=-1)
    w_r_hi = w_r.astype(BF16)
    w_r_lo = (w_r - w_r_hi.astype(F32)).astype(BF16)
    b_r = jnp.concatenate([b_rg, b_rf, jnp.zeros((depth, pad), F32)], axis=-1).reshape(depth, 1, LANES)

    b_gate4 = b_gate.reshape(depth, -1, 1, d)
    ada = _ada(c, w_ada, b_ada).reshape(depth, b, 1, 6 * d)
    x2 = x.reshape(t, d)
    h = _modulate(x2, ada[0], s)

    for l in range(depth):
        lambda_init = 0.8 - 0.6 * float(np.exp(-0.3 * l))
        ada_l = ada[l]
        n_bc = w_in.shape[2] - 3 * QKV_A
        qkv_groups = _proj_a(h, w_in16, l, b, s)
        proj_bc = _matmul(h, w_in16, l, 3 * QKV_A, n_bc).reshape(b, s, n_bc)
        gates = _gates(h, w_gate16, b_gate4, l)
        outs_a, lses_a = _mixer_a(qkv_groups)
        y_b = _mixer_b(proj_bc, lam_vecs[l], subln_g[l], lambda_init)
        y_c = _mixer_c(proj_bc, sinks[l])
        merged = _merge(outs_a, lses_a, y_b.reshape(t, -1), y_c.reshape(t, -1), gates, w_branch16, l, s)
        x2, route, counts = _wo_ln_route(merged, w_o16, l, x2, ada_l, ln_g[l, 0:1], ln_b[l, 0:1],
                                         w_r_hi[l], w_r_lo[l], b_r[l], s, alpha)
        starts_lanes, plan = _dispatch_plan(counts, 2 * t)
        pos = _positions(route, starts_lanes)
        pos0, pos1 = pos[0], pos[1]
        xs = _dispatch(x2, ada_l, pos0, pos1, s)
        ys = _experts(xs, w1, w3, w2, l, plan)
        ada_next = ada[min(l + 1, depth - 1)]
        x2, h = _combine(ys, pos0, pos1, route, x2, ada_l, ada_next, ln_g[l, 1:2], ln_b[l, 1:2], s, alpha)
    return x2.reshape(b, s, d)
```

```python
import functools

import numpy as np
import jax
import jax.numpy as jnp
from jax import lax
from jax.experimental import pallas as pl
from jax.experimental.pallas import tpu as pltpu

F32 = jnp.float32
BF16 = jnp.bfloat16
HIGHEST = lax.Precision.HIGHEST

BLOCK = 128
DILATED_GROUPS = ((128, 1), (512, 4), (2048, 16))
N_GROUPS_A = 3
HEADS_A = 4
HEAD_DIM_A = 128
GROUP_WIDTH_A = HEADS_A * HEAD_DIM_A
QKV_A = N_GROUPS_A * GROUP_WIDTH_A
HEADS_B = 4
HEAD_DIM_B = 64
HEADS_C = 8
KV_HEADS_C = 2
GQA_GROUP_C = HEADS_C // KV_HEADS_C
HEAD_DIM_C = 64
WINDOW_C = 128
BRANCH_WIDTH = 512
N_EXPERT_GROUPS = 4
EXPERTS_PER_GROUP = 8
N_EXPERTS = N_EXPERT_GROUPS * EXPERTS_PER_GROUP
LN_EPS = 1e-5
NEG = -1e30

LANES = 128
SUBLANES = 8
ROUTE_LANE0 = N_EXPERT_GROUPS
VMEM_LIMIT = 56 * 1024 * 1024
EXPERT_VMEM_LIMIT = 60 * 1024 * 1024

EXPERT_TILE = 512
ROW_CHUNK = 16
ROW_UNROLL = 8


def _alibi_slopes(n):
    return [float(v) for v in 2.0 ** (-8.0 * np.arange(1, n + 1, dtype=np.float32) / n)]


def _params(*sem):
    return pltpu.CompilerParams(dimension_semantics=sem, vmem_limit_bytes=VMEM_LIMIT)


def _ada_kernel(c_ref, w_ref, b_ref, o_ref):
    c = c_ref[...]
    cond = c * jax.nn.sigmoid(c)
    o_ref[0] = jnp.dot(cond, w_ref[0], preferred_element_type=F32, precision=HIGHEST) + b_ref[0]


def _ada(c, w_ada, b_ada):
    depth, d, n = w_ada.shape
    bsz = c.shape[0]
    tn = 1024
    return pl.pallas_call(
        _ada_kernel,
        grid=(depth, n // tn),
        in_specs=[pl.BlockSpec((bsz, d), lambda l, j: (0, 0)),
                  pl.BlockSpec((1, d, tn), lambda l, j: (l, 0, j)),
                  pl.BlockSpec((1, 1, tn), lambda l, j: (l, 0, j))],
        out_specs=pl.BlockSpec((1, bsz, tn), lambda l, j: (l, 0, j)),
        out_shape=jax.ShapeDtypeStruct((depth, bsz, n), F32),
        compiler_params=_params("arbitrary", "arbitrary"),
        name="ada",
    )(c, w_ada, b_ada.reshape(depth, 1, n))


def _modulate_kernel(x_ref, sh_ref, sc_ref, o_ref):
    o_ref[...] = (x_ref[...] * (1.0 + sc_ref[0]) + sh_ref[0]).astype(o_ref.dtype)


def _modulate(x2, ada_l, seq, tm=512):
    t, d = x2.shape
    per_b = seq // tm
    return pl.pallas_call(
        _modulate_kernel,
        grid=(t // tm,),
        in_specs=[pl.BlockSpec((tm, d), lambda i: (i, 0)),
                  pl.BlockSpec((1, 1, d), lambda i: (i // per_b, 0, 0)),
                  pl.BlockSpec((1, 1, d), lambda i: (i // per_b, 0, 1))],
        out_specs=pl.BlockSpec((tm, d), lambda i: (i, 0)),
        out_shape=jax.ShapeDtypeStruct((t, d), BF16),
        compiler_params=_params("arbitrary"),
        name="modulate",
    )(x2, ada_l, ada_l)


def _mm_kernel(x_ref, w_ref, o_ref):
    o_ref[...] = jnp.dot(x_ref[...], w_ref[0], preferred_element_type=F32).astype(o_ref.dtype)


def _matmul(x2, w, layer, col0, n, tm=2048, tn=768):
    t, d = x2.shape
    tm = min(tm, t)
    c0 = col0 // tn
    assert col0 % tn == 0 and n % tn == 0
    return pl.pallas_call(
        _mm_kernel,
        grid=(t // tm, n // tn),
        in_specs=[pl.BlockSpec((tm, d), lambda i, j: (i, 0)),
                  pl.BlockSpec((1, d, tn), lambda i, j: (layer, 0, c0 + j))],
        out_specs=pl.BlockSpec((tm, tn), lambda i, j: (i, j)),
        out_shape=jax.ShapeDtypeStruct((t, n), BF16),
        compiler_params=_params("arbitrary", "arbitrary"),
        name="proj",
    )(x2, w)


def _proj_a_kernel(x_ref, w_ref, o0_ref, o1_ref, o2_ref, buf):
    g = pl.program_id(1) % N_GROUPS_A
    acc = jnp.dot(x_ref[...], w_ref[0], preferred_element_type=F32)
    tm = acc.shape[0]
    for gi, (o_ref, (_, r)) in enumerate(zip((o0_ref, o1_ref, o2_ref), DILATED_GROUPS)):
        @pl.when(g == gi)
        def _(o_ref=o_ref, r=r):
            if r == 1:
                o_ref[0, 0] = acc.astype(o_ref.dtype)
            else:
                for c in range(GROUP_WIDTH_A // LANES):
                    buf[c] = acc[:, c * LANES:(c + 1) * LANES]
                for rho in range(r):
                    for c in range(GROUP_WIDTH_A // LANES):
                        o_ref[0, rho, :, c * LANES:(c + 1) * LANES] = (
                            buf[c, pl.ds(rho, tm // r, stride=r), :].astype(o_ref.dtype))


def _proj_a(x2, w, layer, bsz, seq, tm=2048):
    t, d = x2.shape
    tm = min(tm, seq)
    per_b = seq // tm
    w_g = GROUP_WIDTH_A
    n_tiles = 3 * N_GROUPS_A

    def out_spec(g, r):
        return pl.BlockSpec((1, r, tm // r, w_g),
                            lambda i, j: (i // per_b, 0, i % per_b, jnp.maximum(j - g, 0) // N_GROUPS_A))

    return pl.pallas_call(
        _proj_a_kernel,
        grid=(t // tm, n_tiles),
        in_specs=[pl.BlockSpec((tm, d), lambda i, j: (i, 0)),
                  pl.BlockSpec((1, d, w_g), lambda i, j: (layer, 0, j))],
        out_specs=[out_spec(g, r) for g, (_, r) in enumerate(DILATED_GROUPS)],
        out_shape=[jax.ShapeDtypeStruct((bsz, r, seq // r, 3 * w_g), BF16) for _, r in DILATED_GROUPS],
        scratch_shapes=[pltpu.VMEM((w_g // LANES, tm, LANES), F32)],
        compiler_params=_params("arbitrary", "arbitrary"),
        name="proj_a",
    )(x2, w)


def _gate_kernel(x_ref, w_ref, b_ref, o_ref):
    acc = jnp.dot(x_ref[...], w_ref[0, 0], preferred_element_type=F32) + b_ref[0, 0]
    o_ref[...] = jax.nn.sigmoid(acc).astype(o_ref.dtype)


def _gates(x2, w_gate, b_gate, layer, tm=2048, tn=1024):
    t, d = x2.shape
    _, nbr, _, n = w_gate.shape
    tm = min(tm, t)
    per = n // tn
    return pl.pallas_call(
        _gate_kernel,
        grid=(t // tm, nbr * per),
        in_specs=[pl.BlockSpec((tm, d), lambda i, j: (i, 0)),
                  pl.BlockSpec((1, 1, d, tn), lambda i, j: (layer, j // per, 0, j % per)),
                  pl.BlockSpec((1, 1, 1, tn), lambda i, j: (layer, j // per, 0, j % per))],
        out_specs=pl.BlockSpec((tm, tn), lambda i, j: (i, j)),
        out_shape=jax.ShapeDtypeStruct((t, nbr * n), BF16),
        compiler_params=_params("arbitrary", "arbitrary"),
        name="gates",
    )(x2, w_gate, b_gate)


def _attn_a_kernel(q_ref, kp_ref, kc_ref, vp_ref, vc_ref, o_ref, lse_ref, kbuf, vbuf, *, slopes, dist_unit, nb):
    n = pl.program_id(1)
    kbuf[0:BLOCK, :] = kp_ref[0]
    kbuf[BLOCK:, :] = kc_ref[0]
    vbuf[0:BLOCK, :] = vp_ref[0]
    vbuf[BLOCK:, :] = vc_ref[0]
    row = lax.broadcasted_iota(jnp.int32, (BLOCK, 2 * BLOCK), 0)
    col = lax.broadcasted_iota(jnp.int32, (BLOCK, 2 * BLOCK), 1)
    dist = row + BLOCK - col
    band = (dist >= 0) & (dist <= BLOCK)
    first_lim = jnp.where(n > 0, 0, BLOCK)
    band_first = band & (col >= first_lim)
    distf = dist.astype(F32) * float(dist_unit)
    scale = HEAD_DIM_A ** -0.5
    for h in range(HEADS_A):
        hs = slice(h * HEAD_DIM_A, (h + 1) * HEAD_DIM_A)
        bias = -slopes[h] * distf
        for i in range(nb // BLOCK):
            rows = slice(i * BLOCK, (i + 1) * BLOCK)
            q = q_ref[0, rows, hs]
            k = kbuf[i * BLOCK:(i + 2) * BLOCK, hs]
            v = vbuf[i * BLOCK:(i + 2) * BLOCK, hs]
            s = lax.dot_general(q, k, (((1,), (1,)), ((), ())), preferred_element_type=F32) * scale + bias
            s = jnp.where(band_first if i == 0 else band, s, NEG)
            m = jnp.max(s, axis=-1, keepdims=True)
            p = jnp.exp(s - m)
            l = jnp.sum(p, axis=-1, keepdims=True)
            o = jnp.dot(p.astype(BF16), v, preferred_element_type=F32) / l
            o_ref[0, rows, hs] = o.astype(o_ref.dtype)
            lse_ref[0, rows, h:h + 1] = m + jnp.log(l)


def _attn_a_group(qkv, cols, slopes, dist_unit):
    nseq, l, _ = qkv.shape
    nb = min(1024, l)
    sub = nb // BLOCK
    cq, ck, cv = cols
    w = GROUP_WIDTH_A
    kern = functools.partial(_attn_a_kernel, slopes=slopes, dist_unit=dist_unit, nb=nb)
    prev = lambda b, n: jnp.maximum(n * sub - 1, 0)
    return pl.pallas_call(
        kern,
        grid=(nseq, l // nb),
        in_specs=[pl.BlockSpec((1, nb, w), lambda b, n: (b, n, cq)),
                  pl.BlockSpec((1, BLOCK, w), lambda b, n: (b, prev(b, n), ck)),
                  pl.BlockSpec((1, nb, w), lambda b, n: (b, n, ck)),
                  pl.BlockSpec((1, BLOCK, w), lambda b, n: (b, prev(b, n), cv)),
                  pl.BlockSpec((1, nb, w), lambda b, n: (b, n, cv))],
        out_specs=[pl.BlockSpec((1, nb, w), lambda b, n: (b, n, 0)),
                   pl.BlockSpec((1, nb, HEADS_A), lambda b, n: (b, n, 0))],
        out_shape=[jax.ShapeDtypeStruct((nseq, l, w), BF16),
                   jax.ShapeDtypeStruct((nseq, l, HEADS_A), F32)],
        scratch_shapes=[pltpu.VMEM((BLOCK + nb, w), BF16), pltpu.VMEM((BLOCK + nb, w), BF16)],
        compiler_params=_params("arbitrary", "arbitrary"),
        name="attn_a",
    )(qkv, qkv, qkv, qkv, qkv)


def _mixer_a(qkv_groups):
    slopes = _alibi_slopes(N_GROUPS_A * HEADS_A)
    outs, lses = [], []
    for g, (window, r) in enumerate(DILATED_GROUPS):
        assert window // r == BLOCK
        b, _, l, w = qkv_groups[g].shape
        o, lse = _attn_a_group(qkv_groups[g].reshape(b * r, l, w), (0, 1, 2), slopes[g * HEADS_A:(g + 1) * HEADS_A], r)
        outs.append(o.reshape(b, r, l, GROUP_WIDTH_A))
        lses.append(lse.reshape(b, r, l, HEADS_A).transpose(0, 2, 1, 3).reshape(b * l * r, HEADS_A))
    return outs, lses


def _attn_b_kernel(q_ref, k_ref, v_ref, lv_ref, g_ref, o_ref, kt_buf, s_buf, m_buf, acc_buf, *,
                   slopes, lambda_init, tq):
    h = pl.program_id(1)
    qi = pl.program_id(2)
    d = HEAD_DIM_B
    nk = kt_buf.shape[1]
    nt = (((1,), (1,)), ((), ()))

    @pl.when(qi == 0)
    def _():
        eye = (lax.broadcasted_iota(jnp.int32, (d, d), 0) == lax.broadcasted_iota(jnp.int32, (d, d), 1)).astype(BF16)

        def transpose(c, carry):
            k = k_ref[0, pl.ds(pl.multiple_of(c * tq, tq), tq), :]
            for j in range(2):
                kt = lax.dot_general(eye, k[:, j * d:(j + 1) * d], nt, preferred_element_type=F32)
                kt_buf[j, c] = kt.astype(BF16)
            return carry

        lax.fori_loop(0, nk, transpose, 0)

    slope = jnp.float32(slopes[0])
    for j in range(1, HEADS_B):
        slope = jnp.where(h == j, jnp.float32(slopes[j]), slope)
    q = q_ref[0]
    qs = [(q[:, j * d:(j + 1) * d].astype(F32) * (d ** -0.5)).astype(BF16) for j in range(2)]
    colf = lax.broadcasted_iota(jnp.int32, (1, tq), 1).astype(F32)
    nl = tq // LANES

    def lane_fold(x, op, acc):
        for c in range(nl):
            acc = op(acc, x[:, c * LANES:(c + 1) * LANES])
        return acc

    m_buf[...] = jnp.full(m_buf.shape, NEG, F32)

    def scores(kb, masked):
        bias = slope * (colf + ((kb - qi) * tq).astype(F32))
        for j in range(2):
            s = jnp.dot(qs[j], kt_buf[j, kb], preferred_element_type=F32) + bias
            if masked:
                row = lax.broadcasted_iota(jnp.int32, (tq, tq), 0)
                col = lax.broadcasted_iota(jnp.int32, (tq, tq), 1)
                s = jnp.where(col <= row, s, NEG)
            s_buf[j, kb] = s
            m_buf[j] = lane_fold(s, jnp.maximum, m_buf[j])

    def scores_body(kb, carry):
        scores(kb, False)
        return carry

    lax.fori_loop(0, qi, scores_body, 0)
    scores(qi, True)
    ms = [jnp.max(m_buf[j], axis=-1, keepdims=True) for j in range(2)]

    acc_buf[...] = jnp.zeros_like(acc_buf)
    ones_col = (lax.broadcasted_iota(jnp.int32, (tq, 2 * d), 1) == 0).astype(BF16)

    def weights(kb, carry):
        v = v_ref[0, pl.ds(pl.multiple_of(kb * tq, tq), tq), :]
        v1 = jnp.concatenate([v, ones_col], axis=1)
        for j in range(2):
            p = jnp.exp(s_buf[j, kb] - ms[j])
            acc_buf[j] = acc_buf[j] + jnp.dot(p.astype(BF16), v1, preferred_element_type=F32)
        return carry

    lax.fori_loop(0, qi + 1, weights, 0)
    a0, a1 = acc_buf[0, :, 0:2 * d], acc_buf[1, :, 0:2 * d]
    l0, l1 = acc_buf[0, :, 2 * d:2 * d + 1], acc_buf[1, :, 2 * d:2 * d + 1]
    lv = lv_ref[...]
    lam = (jnp.exp(jnp.sum(lv[0:1] * lv[1:2], keepdims=True)) - jnp.exp(jnp.sum(lv[2:3] * lv[3:4], keepdims=True))
           + lambda_init)
    o = a0 / l0 - lam * (a1 / l1)
    o = o * lax.rsqrt(jnp.mean(o * o, axis=-1, keepdims=True) + LN_EPS)
    o = o * g_ref[...] * (1.0 - lambda_init)
    o_ref[0] = o.astype(o_ref.dtype)


def _mixer_b(proj_bc, lam_vecs, subln_g, lambda_init, tq=512):
    b, s, _ = proj_bc.shape
    w = 2 * HEAD_DIM_B
    nk = s // tq
    kern = functools.partial(_attn_b_kernel, slopes=_alibi_slopes(HEADS_B), lambda_init=lambda_init, tq=tq)
    return pl.pallas_call(
        kern,
        grid=(b, HEADS_B, nk),
        in_specs=[pl.BlockSpec((1, tq, w), lambda bi, h, i: (bi, i, h)),
                  pl.BlockSpec((1, s, w), lambda bi, h, i: (bi, 0, HEADS_B + h)),
                  pl.BlockSpec((1, s, w), lambda bi, h, i: (bi, 0, 2 * HEADS_B + h)),
                  pl.BlockSpec(lam_vecs.shape, lambda bi, h, i: (0, 0)),
                  pl.BlockSpec((1, w), lambda bi, h, i: (0, 0))],
        out_specs=pl.BlockSpec((1, tq, w), lambda bi, h, i: (bi, i, h)),
        out_shape=jax.ShapeDtypeStruct((b, s, HEADS_B * w), BF16),
        scratch_shapes=[pltpu.VMEM((2, nk, HEAD_DIM_B, tq), BF16), pltpu.VMEM((2, nk, tq, tq), F32),
                        pltpu.VMEM((2, tq, LANES), F32), pltpu.VMEM((2, tq, 2 * w), F32)],
        compiler_params=_params("arbitrary", "arbitrary", "arbitrary"),
        name="attn_b",
    )(proj_bc, proj_bc, proj_bc, lam_vecs, subln_g.reshape(1, w))


def _attn_c_kernel(sink_ref, q_ref, kp_ref, kc_ref, vp_ref, vc_ref, o_ref, kbuf, vbuf, *, slopes, nb):
    n = pl.program_id(1)
    d = HEAD_DIM_C
    half = lax.broadcasted_iota(jnp.int32, (1, 2 * d), 1) < d

    for j in range(KV_HEADS_C):
        for src_p, src_c, dst in ((kp_ref, kc_ref, kbuf), (vp_ref, vc_ref, vbuf)):
            xp = src_p[0, :, j * d:(j + 1) * d]
            xc = src_c[0, :, j * d:(j + 1) * d]
            dst[j, 0:BLOCK, :] = jnp.concatenate([xp, xp], axis=1)
            dst[j, BLOCK:, :] = jnp.concatenate([xc, xc], axis=1)

    row = lax.broadcasted_iota(jnp.int32, (BLOCK, 2 * BLOCK), 0)
    col = lax.broadcasted_iota(jnp.int32, (BLOCK, 2 * BLOCK), 1)
    dist = row + BLOCK - col
    band = (dist >= 0) & (dist <= WINDOW_C - 1)
    first_lim = jnp.where(n > 0, 0, BLOCK)
    band_first = band & (col >= first_lim)
    distf = dist.astype(F32)
    scale = d ** -0.5
    zero = jnp.zeros((), BF16)
    for hp in range(HEADS_C // 2):
        kv = (2 * hp) // GQA_GROUP_C
        for i in range(nb // BLOCK):
            rows = slice(i * BLOCK, (i + 1) * BLOCK)
            qp = q_ref[0, rows, hp * 2 * d:(hp + 1) * 2 * d]
            k = kbuf[kv, i * BLOCK:(i + 2) * BLOCK, :]
            v = vbuf[kv, i * BLOCK:(i + 2) * BLOCK, :]
            outs = []
            for e in range(2):
                h = 2 * hp + e
                q = jnp.where(half if e == 0 else ~half, qp, zero)
                s = lax.dot_general(q, k, (((1,), (1,)), ((), ())), preferred_element_type=F32) * scale - slopes[h] * distf
                s = jnp.where(band_first if i == 0 else band, s, NEG)
                sink = sink_ref[h]
                m = jnp.maximum(jnp.max(s, axis=-1, keepdims=True), sink)
                p = jnp.exp(s - m)
                denom = jnp.sum(p, axis=-1, keepdims=True) + jnp.exp(sink - m)
                outs.append(jnp.dot(p.astype(BF16), v, preferred_element_type=F32) / denom)
            o_ref[0, rows, hp * 2 * d:(hp + 1) * 2 * d] = jnp.where(half, outs[0], outs[1]).astype(o_ref.dtype)


def _mixer_c(proj_bc, sinks):
    b, s, _ = proj_bc.shape
    nb = min(512, s)
    sub = nb // BLOCK
    wq = HEADS_C * HEAD_DIM_C
    wkv = KV_HEADS_C * HEAD_DIM_C
    q0 = 3 * HEADS_B * 2 * HEAD_DIM_B
    cq, ck, cv = q0 // wq, (q0 + wq) // wkv, (q0 + wq + wkv) // wkv
    kern = functools.partial(_attn_c_kernel, slopes=_alibi_slopes(HEADS_C), nb=nb)
    prev = lambda n: jnp.maximum(n * sub - 1, 0)
    return pl.pallas_call(
        kern,
        grid=(b, s // nb),
        in_specs=[pl.BlockSpec(memory_space=pltpu.SMEM),
                  pl.BlockSpec((1, nb, wq), lambda bi, n: (bi, n, cq)),
                  pl.BlockSpec((1, BLOCK, wkv), lambda bi, n: (bi, prev(n), ck)),
                  pl.BlockSpec((1, nb, wkv), lambda bi, n: (bi, n, ck)),
                  pl.BlockSpec((1, BLOCK, wkv), lambda bi, n: (bi, prev(n), cv)),
                  pl.BlockSpec((1, nb, wkv), lambda bi, n: (bi, n, cv))],
        out_specs=pl.BlockSpec((1, nb, wq), lambda bi, n: (bi, n, 0)),
        out_shape=jax.ShapeDtypeStruct((b, s, wq), BF16),
        scratch_shapes=[pltpu.VMEM((KV_HEADS_C, BLOCK + nb, 2 * HEAD_DIM_C), BF16),
                        pltpu.VMEM((KV_HEADS_C, BLOCK + nb, 2 * HEAD_DIM_C), BF16)],
        compiler_params=_params("arbitrary", "arbitrary"),
        name="attn_c",
    )(sinks, proj_bc, proj_bc, proj_bc, proj_bc, proj_bc)


def _merge_kernel(o1_ref, o2_ref, o3_ref, l1_ref, l2_ref, l3_ref, yb_ref, yc_ref, g_ref, wb_ref, out_ref, obuf):
    la, lb, lc = l1_ref[...], l2_ref[...], l3_ref[...]
    m = jnp.maximum(jnp.maximum(la, lb), lc)
    ea, eb, ec = jnp.exp(la - m), jnp.exp(lb - m), jnp.exp(lc - m)
    den = ea + eb + ec
    wts = (ea / den, eb / den, ec / den)
    tm = out_ref.shape[0]
    for gi, (o_ref, (_, r)) in enumerate(zip((o1_ref, o2_ref, o3_ref), DILATED_GROUPS)):
        for rho in range(r):
            for h in range(HEADS_A):
                hs = slice(h * HEAD_DIM_A, (h + 1) * HEAD_DIM_A)
                if r == 1:
                    obuf[gi, h] = o_ref[0, 0, :, hs].astype(F32)
                else:
                    obuf[gi, h, pl.ds(rho, tm // r, stride=r), :] = o_ref[0, rho, :, hs].astype(F32)
    parts = []
    for h in range(HEADS_A):
        y = wts[0][:, h:h + 1] * obuf[0, h] + wts[1][:, h:h + 1] * obuf[1, h] + wts[2][:, h:h + 1] * obuf[2, h]
        parts.append(y.astype(BF16))
    ya = jnp.concatenate(parts, axis=1)
    dm = out_ref.shape[1]
    acc = g_ref[:, 0:dm].astype(F32) * jnp.dot(ya, wb_ref[0, 0], preferred_element_type=F32)
    acc = acc + g_ref[:, dm:2 * dm].astype(F32) * jnp.dot(yb_ref[...], wb_ref[0, 1], preferred_element_type=F32)
    acc = acc + g_ref[:, 2 * dm:3 * dm].astype(F32) * jnp.dot(yc_ref[...], wb_ref[0, 2], preferred_element_type=F32)
    out_ref[...] = acc.astype(out_ref.dtype)


def _merge(outs_a, lses_a, y_b, y_c, gates, w_branch, layer, seq, tm=512):
    t = y_b.shape[0]
    dm = w_branch.shape[3]
    per_b = seq // tm
    wide = lambda w: pl.BlockSpec((tm, w), lambda i: (i, 0))
    group = lambda r: pl.BlockSpec((1, r, tm // r, GROUP_WIDTH_A), lambda i: (i // per_b, 0, i % per_b, 0))
    return pl.pallas_call(
        _merge_kernel,
        grid=(t // tm,),
        in_specs=[group(r) for _, r in DILATED_GROUPS] + [wide(HEADS_A)] * 3 + [wide(BRANCH_WIDTH)] * 2
                 + [wide(3 * dm)] + [pl.BlockSpec((1,) + w_branch.shape[1:], lambda i: (layer, 0, 0, 0))],
        out_specs=wide(dm),
        out_shape=jax.ShapeDtypeStruct((t, dm), BF16),
        scratch_shapes=[pltpu.VMEM((N_GROUPS_A, HEADS_A, tm, HEAD_DIM_A), F32)],
        compiler_params=_params("arbitrary"),
        name="merge",
    )(*outs_a, *lses_a, y_b, y_c, gates, w_branch)


def _layer_norm(z, g, b):
    mu = jnp.mean(z, axis=-1, keepdims=True)
    zc = z - mu
    var = jnp.mean(zc * zc, axis=-1, keepdims=True)
    return zc * lax.rsqrt(var + LN_EPS) * g + b


def _wo_ln_route_kernel(a_ref, wo_ref, x_ref, g1_ref, sh_ref, sc_ref, lng_ref, lnb_ref, whi_ref, wlo_ref, br_ref,
                        xo_ref, route_ref, cnt_ref, run_ref, mix_buf, hhi_buf, hlo_buf, before_buf, *, alpha, tm):
    @pl.when(pl.program_id(0) == 0)
    def _():
        run_ref[...] = jnp.zeros_like(run_ref)
        r_i = lax.broadcasted_iota(jnp.int32, (tm, tm), 0)
        c_i = lax.broadcasted_iota(jnp.int32, (tm, tm), 1)
        before_buf[...] = jnp.where(c_i < r_i, 1.0, 0.0).astype(BF16)

    mix_buf[...] = jnp.dot(a_ref[...], wo_ref[0], preferred_element_type=F32)

    def rows(r, carry):
        rs = pl.ds(pl.multiple_of(r * ROW_CHUNK, ROW_CHUNK), ROW_CHUNK)
        xn = _layer_norm(alpha * x_ref[rs, :] + g1_ref[0] * mix_buf[rs, :], lng_ref[...], lnb_ref[...])
        xo_ref[rs, :] = xn
        h = xn * (1.0 + sc_ref[0]) + sh_ref[0]
        hi = h.astype(BF16)
        hhi_buf[rs, :] = hi
        hlo_buf[rs, :] = (h - hi.astype(F32)).astype(BF16)
        return carry

    lax.fori_loop(0, tm // ROW_CHUNK, rows, 0, unroll=ROW_UNROLL)

    hhi = hhi_buf[...]
    logits = (jnp.dot(hhi, whi_ref[...], preferred_element_type=F32)
              + jnp.dot(hlo_buf[...], whi_ref[...], preferred_element_type=F32)
              + jnp.dot(hhi, wlo_ref[...], preferred_element_type=F32) + br_ref[...])
    lane = lax.broadcasted_iota(jnp.int32, (tm, LANES), 1).astype(F32)
    far = float(4 * LANES)
    gl = jnp.where(lane < N_EXPERT_GROUPS, logits, NEG)
    gm = jnp.max(gl, axis=-1, keepdims=True)
    g_val = 1.0 / jnp.sum(jnp.exp(gl - gm), axis=-1, keepdims=True)
    g_idx = jnp.min(jnp.where(gl == gm, lane, far), axis=-1, keepdims=True)
    lo = ROUTE_LANE0 + EXPERTS_PER_GROUP * g_idx
    fl = jnp.where((lane >= lo) & (lane < lo + EXPERTS_PER_GROUP), logits, NEG)
    m1 = jnp.max(fl, axis=-1, keepdims=True)
    i1 = jnp.min(jnp.where(fl == m1, lane, far), axis=-1, keepdims=True)
    fl2 = jnp.where(lane == i1, NEG, fl)
    m2 = jnp.max(fl2, axis=-1, keepdims=True)
    i2 = jnp.min(jnp.where(fl2 == m2, lane, far), axis=-1, keepdims=True)
    e = jnp.exp(m2 - m1)
    w1 = g_val / (1.0 + e)
    w2 = g_val * e / (1.0 + e)

    oh1 = lane == i1
    oh2 = lane == i2
    c = jnp.where(oh1 | oh2, 1.0, 0.0)
    tot = jnp.dot(before_buf[...], c.astype(BF16), preferred_element_type=F32) + run_ref[...]
    rank1 = jnp.sum(jnp.where(oh1, tot, 0.0), axis=-1, keepdims=True)
    rank2 = jnp.sum(jnp.where(oh2, tot, 0.0), axis=-1, keepdims=True)
    run_ref[...] = run_ref[...] + jnp.sum(c, axis=0, keepdims=True)
    cnt_ref[...] = run_ref[...]

    l8 = lax.broadcasted_iota(jnp.int32, (tm, 8), 1)
    e1 = i1 - ROUTE_LANE0
    e2 = i2 - ROUTE_LANE0
    route = jnp.where(l8 == 0, w1, jnp.where(l8 == 1, w2, jnp.where(l8 == 2, e1, jnp.where(l8 == 3, e2,
            jnp.where(l8 == 4, rank1, rank2)))))
    route_ref[...] = route


def _wo_ln_route(merged, w_o, layer, x2, ada_l, ln_g, ln_b, w_r_hi, w_r_lo, b_r, seq, alpha, tm=512):
    t, d = x2.shape
    per_b = seq // tm
    vec = lambda k: pl.BlockSpec((1, 1, d), lambda i: (i // per_b, 0, k))
    row = lambda w: pl.BlockSpec((tm, w), lambda i: (i, 0))
    full = lambda a: pl.BlockSpec(a.shape, lambda i: (0,) * a.ndim)
    kern = functools.partial(_wo_ln_route_kernel, alpha=alpha, tm=tm)
    return pl.pallas_call(
        kern,
        grid=(t // tm,),
        in_specs=[row(d), pl.BlockSpec((1, d, d), lambda i: (layer, 0, 0)), row(d), vec(2), vec(3), vec(4),
                  full(ln_g), full(ln_b), full(w_r_hi), full(w_r_lo), full(b_r)],
        out_specs=[row(d), row(8), pl.BlockSpec((1, LANES), lambda i: (0, 0))],
        out_shape=[jax.ShapeDtypeStruct((t, d), F32), jax.ShapeDtypeStruct((t, 8), F32),
                   jax.ShapeDtypeStruct((1, LANES), F32)],
        scratch_shapes=[pltpu.VMEM((1, LANES), F32), pltpu.VMEM((tm, d), F32), pltpu.VMEM((tm, d), BF16),
                        pltpu.VMEM((tm, d), BF16), pltpu.VMEM((tm, tm), BF16)],
        compiler_params=_params("arbitrary"),
        name="wo_ln_route",
    )(merged, w_o, x2, ada_l, ada_l, ada_l, ln_g, ln_b, w_r_hi, w_r_lo, b_r)


def _row_copy(src, i, dst, j, sem):
    return pltpu.make_async_copy(src.at[pl.ds(i, 1), :], dst.at[pl.ds(j, 1), :], sem)


def _dispatch_kernel(p0_ref, p1_ref, x_ref, sh_ref, sc_ref, xs_hbm, hbuf, sem, *, tc):
    step = pl.program_id(0)
    slot = step % 2
    stage = hbuf.at[slot]

    def stage_and_issue(k, c):
        base = pl.multiple_of(k * ROW_CHUNK, ROW_CHUNK)
        rs = pl.ds(base, ROW_CHUNK)
        stage[rs, :] = x_ref[rs, :] * (1.0 + sc_ref[0]) + sh_ref[0]
        for r in range(ROW_CHUNK):
            _row_copy(stage, base + r, xs_hbm, p0_ref[base + r], sem.at[slot]).start()
            _row_copy(stage, base + r, xs_hbm, p1_ref[base + r], sem.at[slot]).start()
        return c

    lax.fori_loop(0, tc // ROW_CHUNK, stage_and_issue, 0)

    def drain(which):
        def body(i, c):
            _row_copy(hbuf.at[which], 0, xs_hbm, 0, sem.at[which]).wait()
            return c

        lax.fori_loop(0, 2 * tc, body, 0, unroll=8)

    @pl.when(step > 0)
    def _():
        drain(1 - slot)

    @pl.when(step == pl.num_programs(0) - 1)
    def _():
        drain(slot)


def _dispatch(x2, ada_l, pos0, pos1, seq, tc=256):
    t, d = x2.shape
    per_b = seq // tc
    vec = lambda k: pl.BlockSpec((1, 1, d), lambda i: (i // per_b, 0, k))
    return pl.pallas_call(
        functools.partial(_dispatch_kernel, tc=tc),
        grid=(t // tc,),
        in_specs=[pl.BlockSpec((tc,), lambda i: (i,), memory_space=pltpu.SMEM),
                  pl.BlockSpec((tc,), lambda i: (i,), memory_space=pltpu.SMEM),
                  pl.BlockSpec((tc, d), lambda i: (i, 0)), vec(3), vec(4)],
        out_specs=pl.BlockSpec(memory_space=pl.ANY),
        out_shape=jax.ShapeDtypeStruct((2 * t, d), F32),
        scratch_shapes=[pltpu.VMEM((2, tc, d), F32), pltpu.SemaphoreType.DMA((2,))],
        compiler_params=_params("arbitrary"),
        name="dispatch",
    )(pos0, pos1, x2, ada_l, ada_l)


def _expert_kernel(it_ref, ie_ref, lo_ref, hi_ref, first_ref, nu_ref, x_ref, w1_ref, w3_ref, w2_ref, y_ref,
                   w1b, w3b, w2b, *, tile):
    i = pl.program_id(0)

    @pl.when((i == 0) | (ie_ref[i] != ie_ref[jnp.maximum(i - 1, 0)]))
    def _():
        w1b[...] = w1_ref[0, 0].astype(BF16)
        w3b[...] = w3_ref[0, 0].astype(BF16)
        w2b[...] = w2_ref[0, 0].astype(BF16)

    @pl.when(i < nu_ref[0])
    def _():
        x = x_ref[...].astype(BF16)
        a = jnp.dot(x, w1b[...], preferred_element_type=F32)
        b = jnp.dot(x, w3b[...], preferred_element_type=F32)
        he = (a * jax.nn.sigmoid(a) * b).astype(BF16)
        y = jnp.dot(he, w2b[...], preferred_element_type=F32)
        start = it_ref[i] * tile
        row = start + lax.broadcasted_iota(jnp.int32, (tile, 1), 0)
        mine = (row >= lo_ref[i]) & (row < hi_ref[i])
        whole = (lo_ref[i] <= start) & (hi_ref[i] >= start + tile)

        @pl.when(whole)
        def _():
            y_ref[...] = y

        @pl.when(jnp.logical_not(whole) & (first_ref[i] == 1))
        def _():
            y_ref[...] = jnp.where(mine, y, 0.0)

        @pl.when(jnp.logical_not(whole) & (first_ref[i] == 0))
        def _():
            y_ref[...] = jnp.where(mine, y, y_ref[...])


def _experts(xs, w1, w3, w2, layer, plan):
    n_rows, d = xs.shape
    de = w1.shape[3]
    tile = EXPERT_TILE
    n_items = plan[0].shape[0]
    row_map = lambda i, it, ie, lo, hi, fi, nu: (it[i], 0)
    w_map = lambda i, it, ie, lo, hi, fi, nu: (layer, ie[i], 0, 0)
    grid_spec = pltpu.PrefetchScalarGridSpec(
        num_scalar_prefetch=6,
        grid=(n_items,),
        in_specs=[pl.BlockSpec((tile, d), row_map),
                  pl.BlockSpec((1, 1, d, de), w_map),
                  pl.BlockSpec((1, 1, d, de), w_map),
                  pl.BlockSpec((1, 1, de, d), w_map)],
        out_specs=pl.BlockSpec((tile, d), row_map),
        scratch_shapes=[pltpu.VMEM((d, de), BF16), pltpu.VMEM((d, de), BF16), pltpu.VMEM((de, d), BF16)],
    )
    return pl.pallas_call(
        functools.partial(_expert_kernel, tile=tile),
        grid_spec=grid_spec,
        out_shape=jax.ShapeDtypeStruct((n_rows, d), F32),
        compiler_params=pltpu.CompilerParams(dimension_semantics=("arbitrary",), vmem_limit_bytes=EXPERT_VMEM_LIMIT),
        name="experts",
    )(*plan, xs, w1, w3, w2)


def _combine_kernel(p0_ref, p1_ref, p0n_ref, p1n_ref, ys_hbm, route_ref, x_ref, g2_ref, lng_ref, lnb_ref, sh_ref, sc_ref,
                    xo_ref, h_ref, buf, sem, *, alpha, tc):
    step = pl.program_id(0)
    slot = step % 2

    def issue_rows(pa_ref, pb_ref, s, base, n):
        for r in range(n):
            _row_copy(ys_hbm, pa_ref[base + r], buf.at[s, 0], base + r, sem.at[s]).start()
            _row_copy(ys_hbm, pb_ref[base + r], buf.at[s, 1], base + r, sem.at[s]).start()

    @pl.when(step == 0)
    def _():
        def issue(k, c):
            issue_rows(p0_ref, p1_ref, 0, pl.multiple_of(k * SUBLANES, SUBLANES), SUBLANES)
            return c

        lax.fori_loop(0, tc // SUBLANES, issue, 0)

    def drain(i, c):
        _row_copy(ys_hbm, 0, buf.at[slot, 0], 0, sem.at[slot]).wait()
        return c

    lax.fori_loop(0, 2 * tc, drain, 0, unroll=8)

    def ln_rows(r):
        rs = pl.ds(pl.multiple_of(r * ROW_CHUNK, ROW_CHUNK), ROW_CHUNK)
        w = route_ref[rs, :]
        ffn = w[:, 0:1] * buf[slot, 0, rs, :] + w[:, 1:2] * buf[slot, 1, rs, :]
        xn = _layer_norm(alpha * x_ref[rs, :] + g2_ref[0] * ffn, lng_ref[...], lnb_ref[...])
        xo_ref[rs, :] = xn
        h_ref[rs, :] = (xn * (1.0 + sc_ref[0]) + sh_ref[0]).astype(h_ref.dtype)

    group = ROW_CHUNK * ROW_UNROLL

    @pl.when(step + 1 < pl.num_programs(0))
    def _():
        def body(g, carry):
            issue_rows(p0n_ref, p1n_ref, 1 - slot, pl.multiple_of(g * group, group), group)
            for u in range(ROW_UNROLL):
                ln_rows(g * ROW_UNROLL + u)
            return carry

        lax.fori_loop(0, tc // group, body, 0)

    @pl.when(step + 1 == pl.num_programs(0))
    def _():
        def body(g, carry):
            for u in range(ROW_UNROLL):
                ln_rows(g * ROW_UNROLL + u)
            return carry

        lax.fori_loop(0, tc // group, body, 0)


def _combine(ys, pos0, pos1, route, x2, ada_l, ada_next, ln_g, ln_b, seq, alpha, tc=512):
    t, d = x2.shape
    per_b = seq // tc
    n = t // tc
    vec = lambda k: pl.BlockSpec((1, 1, d), lambda i: (i // per_b, 0, k))
    row = lambda w: pl.BlockSpec((tc, w), lambda i: (i, 0))
    full = lambda a: pl.BlockSpec(a.shape, lambda i: (0,) * a.ndim)
    kern = functools.partial(_combine_kernel, alpha=alpha, tc=tc)
    return pl.pallas_call(
        kern,
        grid=(n,),
        in_specs=[pl.BlockSpec((tc,), lambda i: (i,), memory_space=pltpu.SMEM),
                  pl.BlockSpec((tc,), lambda i: (i,), memory_space=pltpu.SMEM),
                  pl.BlockSpec((tc,), lambda i: (jnp.minimum(i + 1, n - 1),), memory_space=pltpu.SMEM),
                  pl.BlockSpec((tc,), lambda i: (jnp.minimum(i + 1, n - 1),), memory_space=pltpu.SMEM),
                  pl.BlockSpec(memory_space=pl.ANY),
                  row(8), row(d), vec(5), full(ln_g), full(ln_b), vec(0), vec(1)],
        out_specs=[row(d), row(d)],
        out_shape=[jax.ShapeDtypeStruct((t, d), F32), jax.ShapeDtypeStruct((t, d), BF16)],
        scratch_shapes=[pltpu.VMEM((2, 2, tc, d), F32), pltpu.SemaphoreType.DMA((2,))],
        compiler_params=_params("arbitrary"),
        name="combine",
    )(pos0, pos1, pos0, pos1, ys, route, x2, ada_l, ln_g, ln_b, ada_next, ada_next)


def _positions_kernel(route_ref, starts_ref, pos_ref):
    r = route_ref[...]
    tm = r.shape[0]
    li = lax.broadcasted_iota(jnp.int32, (tm, LANES), 1)
    lane = li.astype(F32)
    starts = starts_ref[...]

    def start_of(e):
        return jnp.sum(jnp.where(lane == e + ROUTE_LANE0, starts, 0.0), axis=-1, keepdims=True)

    p1 = start_of(r[:, 2:3]) + r[:, 4:5]
    p2 = start_of(r[:, 3:4]) + r[:, 5:6]
    cols = jnp.where(li == 0, p1, jnp.where(li == 1, p2, 0.0))
    pos_ref[...] = cols.T[0:8, :].astype(jnp.int32)


def _positions(route, starts, tm=2048):
    t = route.shape[0]
    tm = min(tm, t)
    return pl.pallas_call(
        _positions_kernel,
        grid=(t // tm,),
        in_specs=[pl.BlockSpec((tm, 8), lambda i: (i, 0)), pl.BlockSpec((1, LANES), lambda i: (0, 0))],
        out_specs=pl.BlockSpec((8, tm), lambda i: (0, i)),
        out_shape=jax.ShapeDtypeStruct((8, t), jnp.int32),
        compiler_params=_params("arbitrary"),
        name="positions",
    )(route, starts)


def _dispatch_plan(counts, n_rows):
    tile = EXPERT_TILE
    n_items = n_rows // tile + N_EXPERTS
    starts_lanes = jnp.cumsum(counts, axis=1) - counts
    cnt = counts[0, ROUTE_LANE0:ROUTE_LANE0 + N_EXPERTS].astype(jnp.int32)
    ends = jnp.cumsum(cnt)
    starts = ends - cnt
    first_tile = starts // tile
    n_e = jnp.where(cnt > 0, (ends - 1) // tile - first_tile + 1, 0)
    item_ends = jnp.cumsum(n_e)
    n_used = item_ends[-1]
    k = jnp.minimum(jnp.arange(n_items, dtype=jnp.int32), n_used - 1)
    item_expert = jnp.minimum(jnp.sum((item_ends[None, :] <= k[:, None]).astype(jnp.int32), axis=1), N_EXPERTS - 1)
    onehot = (item_expert[:, None] == jnp.arange(N_EXPERTS, dtype=jnp.int32)[None, :]).astype(jnp.int32)
    pick = lambda v: jnp.sum(onehot * v[None, :], axis=1)
    item_tile = pick(first_tile) + k - pick(item_ends - n_e)
    item_first = jnp.concatenate([jnp.ones((1,), jnp.int32), (item_tile[1:] != item_tile[:-1]).astype(jnp.int32)])
    plan = (item_tile, item_expert, pick(starts), pick(ends), item_first, n_used.reshape(1))
    return starts_lanes, tuple(p.astype(jnp.int32) for p in plan)


def kernel(x, c, w_ada, b_ada, w_in, w_gate, b_gate, w_branch, w_o, lam_vecs, subln_g, sinks, ln_g, ln_b,
           w_rg, b_rg, w_rf, b_rf, w1, w3, w2):
    b, s, d = x.shape
    depth = w_ada.shape[0]
    t = b * s
    alpha = (2 * depth) ** 0.25
    assert s % (16 * BLOCK) == 0 and d % LANES == 0

    w_in16 = w_in.astype(BF16)
    w_gate16 = w_gate.astype(BF16)
    w_branch16 = w_branch.astype(BF16)
    w_o16 = w_o.astype(BF16)
    pad = LANES - N_EXPERT_GROUPS - N_EXPERTS
    w_r = jnp.concatenate([w_rg, w_rf, jnp.zeros((depth, d, pad), F32)], axis=-1)
    w_r_hi = w_r.astype(BF16)
    w_r_lo = (w_r - w_r_hi.astype(F32)).astype(BF16)
    b_r = jnp.concatenate([b_rg, b_rf, jnp.zeros((depth, pad), F32)], axis=-1).reshape(depth, 1, LANES)

    b_gate4 = b_gate.reshape(depth, -1, 1, d)
    ada = _ada(c, w_ada, b_ada).reshape(depth, b, 1, 6 * d)
    x2 = x.reshape(t, d)
    h = _modulate(x2, ada[0], s)

    for l in range(depth):
        lambda_init = 0.8 - 0.6 * float(np.exp(-0.3 * l))
        ada_l = ada[l]
        n_bc = w_in.shape[2] - 3 * QKV_A
        qkv_groups = _proj_a(h, w_in16, l, b, s)
        proj_bc = _matmul(h, w_in16, l, 3 * QKV_A, n_bc).reshape(b, s, n_bc)
        gates = _gates(h, w_gate16, b_gate4, l)
        outs_a, lses_a = _mixer_a(qkv_groups)
        y_b = _mixer_b(proj_bc, lam_vecs[l], subln_g[l], lambda_init)
        y_c = _mixer_c(proj_bc, sinks[l])
        merged = _merge(outs_a, lses_a, y_b.reshape(t, -1), y_c.reshape(t, -1), gates, w_branch16, l, s)
        x2, route, counts = _wo_ln_route(merged, w_o16, l, x2, ada_l, ln_g[l, 0:1], ln_b[l, 0:1],
                                         w_r_hi[l], w_r_lo[l], b_r[l], s, alpha)
        starts_lanes, plan = _dispatch_plan(counts, 2 * t)
        pos = _positions(route, starts_lanes)
        pos0, pos1 = pos[0], pos[1]
        xs = _dispatch(x2, ada_l, pos0, pos1, s)
        ys = _experts(xs, w1, w3, w2, l, plan)
        ada_next = ada[min(l + 1, depth - 1)]
        x2, h = _combine(ys, pos0, pos1, route, x2, ada_l, ada_next, ln_g[l, 1:2], ln_b[l, 1:2], s, alpha)
    return x2.reshape(b, s, d)
```
